```python
import math
import jax, jax.numpy as jnp
from jax import lax
import numpy as np

D_MODEL = 2048
BATCH = 2
SEQ = 8192
DEPTH = 1

CONV_WIDTH = D_MODEL // 2
CONV_KERNEL = 31
N_HEADS = 16
HEAD_DIM = D_MODEL // N_HEADS
ROPE_THETA = 500000.0
ROT_FRACTION = 4
IDX_HEADS = 16
IDX_DIM = 64
TOPK_MAX = 256
Q_BLOCK = 128
IDX_SCALE = IDX_DIM ** -0.5 * IDX_HEADS ** -0.5
ATTN_SCALE = HEAD_DIM ** -0.5
NEG_BIG = -1e30
N_GROUPS = 4
EXPERTS_PER_GROUP = 8
N_EXPERTS = N_GROUPS * EXPERTS_PER_GROUP
TOP_EXPERTS = 2
EXPERT_FF = 512
MOE_TOKEN_BLOCK = 128
EPS = 1e-6

IN_WIDTHS = (
    2 * CONV_WIDTH,
    N_HEADS * HEAD_DIM,
    HEAD_DIM,
    HEAD_DIM,
    IDX_HEADS * IDX_DIM,
    IDX_DIM,
    IDX_HEADS,
    D_MODEL,
    D_MODEL,
)
IN_TOTAL = sum(IN_WIDTHS)

kernel_name = "hybrid_conformer_dsa_hmoe"


def rmsnorm(x, g):
    xf = x.astype(jnp.float32)
    y = xf * lax.rsqrt(jnp.mean(xf * xf, axis=-1, keepdims=True) + EPS)
    return (y * g.astype(jnp.float32)).astype(x.dtype)


def layernorm(x, g, b):
    xf = x.astype(jnp.float32)
    mu = jnp.mean(xf, axis=-1, keepdims=True)
    var = jnp.mean(jnp.square(xf - mu), axis=-1, keepdims=True)
    y = (xf - mu) * lax.rsqrt(var + EPS)
    return (y * g.astype(jnp.float32) + b.astype(jnp.float32)).astype(x.dtype)


def rope_tables(positions, rot_dim):
    inv = ROPE_THETA ** (-jnp.arange(0, rot_dim, 2, dtype=jnp.float32) / rot_dim)
    ang = positions.astype(jnp.float32)[..., None] * inv
    return jnp.cos(ang), jnp.sin(ang)


def apply_partial_rope(x, cos, sin):
    half = cos.shape[-1]
    xr = x[..., :2 * half].astype(jnp.float32)
    x1, x2 = xr[..., :half], xr[..., half:]
    c, s = cos[:, :, None, :], sin[:, :, None, :]
    rot = jnp.concatenate([x1 * c - x2 * s, x1 * s + x2 * c], axis=-1).astype(x.dtype)
    return jnp.concatenate([rot, x[..., 2 * half:]], axis=-1)


def split_columns(z):
    points, acc = [], 0
    for w in IN_WIDTHS[:-1]:
        acc += w
        points.append(acc)
    return jnp.split(z, points, axis=-1)


def conformer_conv(u, dw_w, dw_b, ln_g, ln_b, w_pw):
    a, gate = jnp.split(u, 2, axis=-1)
    y = a * jax.nn.sigmoid(gate)
    y = lax.conv_general_dilated(
        y, dw_w[:, None, :].astype(y.dtype), window_strides=(1,),
        padding=[(CONV_KERNEL - 1, 0)],
        dimension_numbers=('NWC', 'WIO', 'NWC'),
        feature_group_count=CONV_WIDTH) + dw_b
    y = jax.nn.silu(layernorm(y, ln_g, ln_b))
    return y @ w_pw


def indexed_sparse_attention(q, k, v, q_idx, k_idx, w_idx):
    B, T = q.shape[0], q.shape[1]
    n_sel = min(TOPK_MAX, T // 4)
    n_blocks = T // Q_BLOCK
    key_pos = jnp.arange(T)

    def block(i):
        start = i * Q_BLOCK
        qb = lax.dynamic_slice_in_dim(q, start, Q_BLOCK, axis=1)
        qib = lax.dynamic_slice_in_dim(q_idx, start, Q_BLOCK, axis=1)
        wb = lax.dynamic_slice_in_dim(w_idx, start, Q_BLOCK, axis=1)
        q_pos = start + jnp.arange(Q_BLOCK)
        causal = key_pos[None, :] <= q_pos[:, None]
        s = jnp.einsum('bqhd,bkd->bqhk', qib, k_idx).astype(jnp.float32)
        score = jnp.einsum('bqh,bqhk->bqk', wb.astype(jnp.float32), jax.nn.relu(s)) * IDX_SCALE
        score = jnp.where(causal[None], score, -jnp.inf)
        _, sel = lax.top_k(score, n_sel)
        k_sel = jax.vmap(lambda kb, ib: kb[ib])(k, sel)
        v_sel = jax.vmap(lambda vb, ib: vb[ib])(v, sel)
        valid = sel <= q_pos[None, :, None]
        logits = jnp.einsum('bqhd,bqkd->bqhk', qb, k_sel).astype(jnp.float32) * ATTN_SCALE
        logits = jnp.where(valid[:, :, None, :], logits, NEG_BIG)
        p = jax.nn.softmax(logits, axis=-1).astype(v.dtype)
        return jnp.einsum('bqhk,bqkd->bqhd', p, v_sel)

    out = lax.map(block, jnp.arange(n_blocks))
    return jnp.moveaxis(out, 0, 1).reshape(B, T, N_HEADS * HEAD_DIM)


def hierarchical_moe(hn, w_rg, b_rg, w_re, b_re, wg, wu, wd):
    B, T, D = hn.shape
    N = B * T
    xf = hn.reshape(N, D)
    g_logits = (xf @ w_rg).astype(jnp.float32) + b_rg.astype(jnp.float32)
    g_prob = jax.nn.softmax(g_logits, axis=-1)
    _, g_idx = lax.top_k(g_logits, 1)
    p_g = jnp.take_along_axis(g_prob, g_idx, axis=-1)
    e_logits = ((xf @ w_re).astype(jnp.float32) + b_re.astype(jnp.float32)).reshape(
        N, N_GROUPS, EXPERTS_PER_GROUP)
    e_in = jnp.take_along_axis(e_logits, g_idx[:, :, None], axis=1)[:, 0]
    e_top, e_idx = lax.top_k(e_in, TOP_EXPERTS)
    e_w = jax.nn.softmax(e_top, axis=-1)
    within = jnp.sum(jax.nn.one_hot(e_idx, EXPERTS_PER_GROUP, dtype=jnp.float32)
                     * e_w[..., None], axis=1)
    comb = (jax.nn.one_hot(g_idx[:, 0], N_GROUPS, dtype=jnp.float32) * p_g)[:, :, None] \
        * within[:, None, :]
    comb = comb.reshape(N, N_EXPERTS).astype(hn.dtype)

    def block(args):
        xb, cb = args
        gate = jnp.einsum('nd,edf->nef', xb, wg)
        up = jnp.einsum('nd,edf->nef', xb, wu)
        act = jax.nn.silu(gate) * up * cb[:, :, None]
        return jnp.einsum('nef,efd->nd', act, wd)

    nb = N // MOE_TOKEN_BLOCK
    y = lax.map(block, (xf.reshape(nb, MOE_TOKEN_BLOCK, D),
                        comb.reshape(nb, MOE_TOKEN_BLOCK, N_EXPERTS)))
    return y.reshape(B, T, D)


def setup_inputs(seed: int = 0) -> dict:
    key = jax.random.key(seed)
    ks = jax.random.split(key, 24)
    f32 = jnp.float32

    def nrm(k, shape, fan_in):
        return jax.random.normal(k, shape, f32) * (fan_in ** -0.5)

    def gain(k, shape):
        return 1.0 + 0.02 * jax.random.normal(k, shape, f32)

    L = DEPTH
    x = jax.random.normal(ks[0], (BATCH, SEQ, D_MODEL), f32)
    offset = jax.random.randint(ks[1], (BATCH, 1), 0, 4096, dtype=jnp.int32)
    positions = (jnp.arange(SEQ, dtype=jnp.int32)[None, :] + offset).astype(jnp.int32)
    return {
        "x": x,
        "positions": positions,
        "attn_norm_g": gain(ks[2], (L, D_MODEL)),
        "w_in": nrm(ks[3], (L, D_MODEL, IN_TOTAL), D_MODEL),
        "conv_dw_w": nrm(ks[4], (L, CONV_KERNEL, CONV_WIDTH), CONV_KERNEL),
        "conv_dw_b": 0.02 * jax.random.normal(ks[5], (L, CONV_WIDTH), f32),
        "conv_ln_g": gain(ks[6], (L, CONV_WIDTH)),
        "conv_ln_b": 0.02 * jax.random.normal(ks[7], (L, CONV_WIDTH), f32),
        "w_conv_out": nrm(ks[8], (L, CONV_WIDTH, D_MODEL), CONV_WIDTH),
        "q_norm_g": gain(ks[9], (L, HEAD_DIM)),
        "k_norm_g": gain(ks[10], (L, HEAD_DIM)),
        "w_attn_o": nrm(ks[11], (L, N_HEADS * HEAD_DIM, D_MODEL), N_HEADS * HEAD_DIM),
        "w_out": nrm(ks[12], (L, D_MODEL, D_MODEL), D_MODEL),
        "ffn_norm_g": gain(ks[13], (L, D_MODEL)),
        "w_router_group": nrm(ks[14], (L, D_MODEL, N_GROUPS), D_MODEL),
        "b_router_group": 0.01 * jax.random.normal(ks[15], (L, N_GROUPS), f32),
        "w_router_expert": nrm(ks[16], (L, D_MODEL, N_EXPERTS), D_MODEL),
        "b_router_expert": 0.01 * jax.random.normal(ks[17], (L, N_EXPERTS), f32),
        "w_exp_gate": nrm(ks[18], (L, N_EXPERTS, D_MODEL, EXPERT_FF), D_MODEL),
        "w_exp_up": nrm(ks[19], (L, N_EXPERTS, D_MODEL, EXPERT_FF), D_MODEL),
        "w_exp_down": nrm(ks[20], (L, N_EXPERTS, EXPERT_FF, D_MODEL), EXPERT_FF),
    }


def reference(x, positions, attn_norm_g, w_in, conv_dw_w, conv_dw_b, conv_ln_g, conv_ln_b,
              w_conv_out, q_norm_g, k_norm_g, w_attn_o, w_out, ffn_norm_g,
              w_router_group, b_router_group, w_router_expert, b_router_expert,
              w_exp_gate, w_exp_up, w_exp_down):
    B, T, _ = x.shape
    cos_a, sin_a = rope_tables(positions, HEAD_DIM // ROT_FRACTION)
    cos_i, sin_i = rope_tables(positions, IDX_DIM // ROT_FRACTION)
    h = x
    for l in range(DEPTH):
        hn = rmsnorm(h, attn_norm_g[l])
        z = hn @ w_in[l]
        u_conv, q, k, v, qi, ki, wi, g_conv, g_attn = split_columns(z)

        y_conv = conformer_conv(u_conv, conv_dw_w[l], conv_dw_b[l], conv_ln_g[l],
                                conv_ln_b[l], w_conv_out[l])

        q = apply_partial_rope(rmsnorm(q.reshape(B, T, N_HEADS, HEAD_DIM), q_norm_g[l]),
                               cos_a, sin_a)
        k = apply_partial_rope(rmsnorm(k.reshape(B, T, 1, HEAD_DIM), k_norm_g[l]),
                               cos_a, sin_a)[:, :, 0]
        qi = apply_partial_rope(qi.reshape(B, T, IDX_HEADS, IDX_DIM), cos_i, sin_i)
        ki = apply_partial_rope(ki.reshape(B, T, 1, IDX_DIM), cos_i, sin_i)[:, :, 0]
        y_attn = indexed_sparse_attention(q, k, v, qi, ki, wi) @ w_attn_o[l]

        mix = jax.nn.sigmoid(g_conv) * y_conv + jax.nn.sigmoid(g_attn) * y_attn
        h = h + mix @ w_out[l]

        hn2 = rmsnorm(h, ffn_norm_g[l])
        h = h + hierarchical_moe(hn2, w_router_group[l], b_router_group[l],
                                 w_router_expert[l], b_router_expert[l],
                                 w_exp_gate[l], w_exp_up[l], w_exp_down[l])
    return h
```

```python
import functools

import jax
import jax.numpy as jnp
from jax import lax
from jax.experimental import pallas as pl
from jax.experimental.pallas import tpu as pltpu

f32 = jnp.float32
bf16 = jnp.bfloat16
i32 = jnp.int32

D_MODEL = 2048
CONV_WIDTH = 1024
CONV_KERNEL = 31
N_HEADS = 16
HEAD_DIM = 128
ROPE_THETA = 500000.0
ROT_FRACTION = 4
IDX_HEADS = 16
IDX_DIM = 64
TOPK_MAX = 256
IDX_SCALE = IDX_DIM ** -0.5 * IDX_HEADS ** -0.5
ATTN_SCALE = HEAD_DIM ** -0.5
N_GROUPS = 4
EXPERTS_PER_GROUP = 8
N_EXPERTS = 32
EXPERT_FF = 512
EPS = 1e-6

LANES = 128
NEG = -1e30
INT_MIN = -2 ** 31

ZW = 9600
Z_U, Z_Q, Z_GC, Z_GA, Z_QI, Z_K, Z_V, Z_KIW = 0, 2048, 4096, 6144, 8192, 9216, 9344, 9472

QB = 128
KB = 512
VMEM_LIMIT = 56 * 1024 * 1024


def _cparams(sem):
    return pltpu.CompilerParams(dimension_semantics=sem, vmem_limit_bytes=VMEM_LIMIT)


def _inproj_kernel(x_ref, g_ref, w_ref, z_ref, hn_ref):
    @pl.when(pl.program_id(1) == 0)
    def _():
        x = x_ref[...]
        ms = jnp.mean(x * x, axis=-1, keepdims=True)
        hn_ref[...] = (x * lax.rsqrt(ms + EPS) * g_ref[...]).astype(bf16)

    z_ref[...] = jnp.dot(hn_ref[...], w_ref[...], preferred_element_type=f32).astype(z_ref.dtype)


def _inproj(x2, g, w_packed, tm=512, tn=1920):
    n = x2.shape[0]
    return pl.pallas_call(
        _inproj_kernel,
        out_shape=jax.ShapeDtypeStruct((n, ZW), bf16),
        grid=(n // tm, ZW // tn),
        in_specs=[pl.BlockSpec((tm, D_MODEL), lambda i, j: (i, 0)),
                  pl.BlockSpec((1, D_MODEL), lambda i, j: (0, 0)),
                  pl.BlockSpec((D_MODEL, tn), lambda i, j: (0, j))],
        out_specs=pl.BlockSpec((tm, tn), lambda i, j: (i, j)),
        scratch_shapes=[pltpu.VMEM((tm, D_MODEL), bf16)],
        compiler_params=_cparams(("parallel", "arbitrary")),
        name="inproj",
    )(x2, g, w_packed)


PT = 512


def _prep_kernel(q_ref, qi_ref, k_ref, v_ref, kiw_ref, pos_ref, inva_ref, invi_ref, qg_ref, kg_ref,
                 qT_ref, qiT_ref, wT_ref, kr_ref, kir_ref, vT_ref):
    lane = lax.broadcasted_iota(i32, (PT, LANES), 1)
    pos = pos_ref[...].astype(f32)
    ang_a = pos * inva_ref[...]
    ca, sa = jnp.cos(ang_a), jnp.sin(ang_a)
    s1a = jnp.where(lane < 16, -sa, 0.0)
    s2a = jnp.where(lane >= 16, sa, 0.0)
    ang_i = pos * invi_ref[...]
    ci, si = jnp.cos(ang_i), jnp.sin(ang_i)
    l64 = lane & 63
    s1i = jnp.where(l64 < 8, -si, 0.0)
    s2i = jnp.where(l64 >= 8, si, 0.0)

    def rope_a(x):
        return x * ca + pltpu.roll(x, 112, 1) * s1a + pltpu.roll(x, 16, 1) * s2a

    def rope_i(x):
        return x * ci + pltpu.roll(x, 120, 1) * s1i + pltpu.roll(x, 8, 1) * s2i

    def rms(x, g):
        ms = jnp.mean(x * x, axis=-1, keepdims=True)
        return x * lax.rsqrt(ms + EPS) * g

    for h in range(N_HEADS):
        qh = q_ref[:, h * HEAD_DIM:(h + 1) * HEAD_DIM].astype(f32)
        qh = rope_a(rms(qh, qg_ref[...])) * ATTN_SCALE
        for sb in range(PT // QB):
            qT_ref[sb, :, h * QB:(h + 1) * QB] = qh[sb * QB:(sb + 1) * QB, :].T.astype(bf16)

    for hp in range(IDX_HEADS // 2):
        x = rope_i(qi_ref[:, hp * LANES:(hp + 1) * LANES].astype(f32))
        for sb in range(PT // QB):
            xt = x[sb * QB:(sb + 1) * QB, :].T
            qiT_ref[sb, :, (2 * hp) * QB:(2 * hp + 1) * QB] = xt[0:IDX_DIM].astype(bf16)
            qiT_ref[sb, :, (2 * hp + 1) * QB:(2 * hp + 2) * QB] = xt[IDX_DIM:2 * IDX_DIM].astype(bf16)

    kiw = kiw_ref[...].astype(f32)
    w = pltpu.roll(kiw, 64, 1) * IDX_SCALE
    for sb in range(PT // QB):
        wT_ref[sb] = w[sb * QB:(sb + 1) * QB, :].T[0:IDX_HEADS]

    kir_ref[...] = jnp.where(lane < IDX_DIM, rope_i(kiw), 0.0).astype(bf16)
    kr_ref[...] = rope_a(rms(k_ref[...].astype(f32), kg_ref[...])).astype(bf16)
    vT_ref[0] = v_ref[...].astype(f32).T.astype(bf16)


def _prep(z, pos2, inv_a, inv_i, qg, kg):
    n = z.shape[0]
    nblk = n // QB
    cb = lambda c, w: c // w
    return pl.pallas_call(
        _prep_kernel,
        out_shape=(jax.ShapeDtypeStruct((nblk, HEAD_DIM, N_HEADS * QB), bf16),
                   jax.ShapeDtypeStruct((nblk, IDX_DIM, IDX_HEADS * QB), bf16),
                   jax.ShapeDtypeStruct((nblk, IDX_HEADS, QB), f32),
                   jax.ShapeDtypeStruct((n, HEAD_DIM), bf16),
                   jax.ShapeDtypeStruct((n, LANES), bf16),
                   jax.ShapeDtypeStruct((n // KB, HEAD_DIM, KB), bf16)),
        grid=(n // PT,),
        in_specs=[pl.BlockSpec((PT, 2048), lambda i: (i, cb(Z_Q, 2048))),
                  pl.BlockSpec((PT, 1024), lambda i: (i, cb(Z_QI, 1024))),
                  pl.BlockSpec((PT, LANES), lambda i: (i, cb(Z_K, LANES))),
                  pl.BlockSpec((PT, LANES), lambda i: (i, cb(Z_V, LANES))),
                  pl.BlockSpec((PT, LANES), lambda i: (i, cb(Z_KIW, LANES))),
                  pl.BlockSpec((PT, 1), lambda i: (i, 0)),
                  pl.BlockSpec((1, LANES), lambda i: (0, 0)),
                  pl.BlockSpec((1, LANES), lambda i: (0, 0)),
                  pl.BlockSpec((1, LANES), lambda i: (0, 0)),
                  pl.BlockSpec((1, LANES), lambda i: (0, 0))],
        out_specs=(pl.BlockSpec((PT // QB, HEAD_DIM, N_HEADS * QB), lambda i: (i, 0, 0)),
                   pl.BlockSpec((PT // QB, IDX_DIM, IDX_HEADS * QB), lambda i: (i, 0, 0)),
                   pl.BlockSpec((PT // QB, IDX_HEADS, QB), lambda i: (i, 0, 0)),
                   pl.BlockSpec((PT, HEAD_DIM), lambda i: (i, 0)),
                   pl.BlockSpec((PT, LANES), lambda i: (i, 0)),
                   pl.BlockSpec((1, HEAD_DIM, KB), lambda i: (i, 0, 0))),
        compiler_params=_cparams(("parallel",)),
        name="prep",
    )(z, z, z, z, z, pos2, inv_a, inv_i, qg, kg)


def _attn_kernel(qT_ref, qiT_ref, wT_ref, kr_ref, kir_ref, vT_ref, o_ref, keys_ref, acc_ref, m_ref, l_ref):
    qb = pl.program_id(1)
    q0 = qb * QB
    nkb = (q0 + QB - 1) // KB + 1
    qidx = q0 + lax.broadcasted_iota(i32, (KB, QB), 1)
    row = lax.broadcasted_iota(i32, (KB, QB), 0)

    def score_body(j, carry):
        k0 = pl.multiple_of(j * KB, KB)
        kblk = kir_ref[pl.ds(k0, KB), 0:IDX_DIM]
        sc = jnp.zeros((KB, QB), f32)
        for hp in range(IDX_HEADS // 2):
            s = jnp.dot(kblk, qiT_ref[:, hp * 2 * QB:(hp + 1) * 2 * QB], preferred_element_type=f32)
            s = jnp.maximum(s, 0.0)
            sc = sc + s[:, 0:QB] * wT_ref[2 * hp:2 * hp + 1, :] + s[:, QB:2 * QB] * wT_ref[2 * hp + 1:2 * hp + 2, :]
        bits = pltpu.bitcast(sc, i32)
        key = bits ^ ((bits >> 31) & jnp.int32(0x7FFFFFFF))
        keys_ref[pl.ds(k0, KB), :] = jnp.where(k0 + row <= qidx, key, jnp.int32(INT_MIN))
        return carry

    lax.fori_loop(0, nkb, score_body, 0)

    lane1 = lax.broadcasted_iota(i32, (1, QB), 1)
    kt = jnp.minimum(TOPK_MAX, q0 + lane1 + 1)

    def bit_body(b, thr):
        cand = thr + lax.shift_left(jnp.int32(1), 31 - b)

        def cnt_body(j, c):
            k0 = pl.multiple_of(j * KB, KB)
            ge = (keys_ref[pl.ds(k0, KB), :] >= cand).astype(i32)
            return c + jnp.sum(ge.reshape(KB // 8, 8, QB), axis=0)

        c8 = lax.fori_loop(0, nkb, cnt_body, jnp.zeros((8, QB), i32))
        cnt = jnp.sum(c8, axis=0, keepdims=True)
        return jnp.where(cnt >= kt, cand, thr)

    thr = lax.fori_loop(0, 32, bit_body, jnp.full((1, QB), INT_MIN, i32))

    m_ref[...] = jnp.full(m_ref.shape, NEG, f32)
    l_ref[...] = jnp.zeros(l_ref.shape, f32)
    acc_ref[...] = jnp.zeros(acc_ref.shape, f32)

    def att_body(j, carry):
        k0 = pl.multiple_of(j * KB, KB)
        kblk = kr_ref[pl.ds(k0, KB), :]
        keyblk = keys_ref[pl.ds(k0, KB), :]
        sel = (keyblk >= thr) & (keyblk != jnp.int32(INT_MIN))
        bias1 = jnp.where(sel, 0.0, NEG).astype(f32)
        bias = jnp.concatenate([bias1, bias1], axis=1)
        vt = vT_ref[j]
        for hp in range(N_HEADS // 2):
            cs = slice(hp * 2 * QB, (hp + 1) * 2 * QB)
            s = jnp.dot(kblk, qT_ref[:, cs], preferred_element_type=f32) + bias
            m_old = m_ref[:, cs]
            m_new = jnp.maximum(m_old, jnp.max(s, axis=0, keepdims=True))
            p = jnp.exp(s - m_new)
            alpha = jnp.exp(m_old - m_new)
            l_ref[:, cs] = l_ref[:, cs] * alpha + jnp.sum(p, axis=0, keepdims=True)
            m_ref[:, cs] = m_new
            pv = jnp.dot(vt, p.astype(bf16), preferred_element_type=f32)
            acc_ref[:, cs] = acc_ref[:, cs] * alpha + pv
        return carry

    lax.fori_loop(0, nkb, att_body, 0)

    out_t = acc_ref[...] / l_ref[...]
    for h in range(N_HEADS):
        o_ref[:, h * HEAD_DIM:(h + 1) * HEAD_DIM] = out_t[:, h * QB:(h + 1) * QB].T.astype(o_ref.dtype)


def _attention(qT, qiT, wT, kr, kir, vT, batch, seq):
    nq = seq // QB
    n = batch * seq
    return pl.pallas_call(
        _attn_kernel,
        out_shape=jax.ShapeDtypeStruct((n, N_HEADS * HEAD_DIM), bf16),
        grid=(batch, nq),
        in_specs=[pl.BlockSpec((None, HEAD_DIM, N_HEADS * QB), lambda b, i: (b * nq + i, 0, 0)),
                  pl.BlockSpec((None, IDX_DIM, IDX_HEADS * QB), lambda b, i: (b * nq + i, 0, 0)),
                  pl.BlockSpec((None, IDX_HEADS, QB), lambda b, i: (b * nq + i, 0, 0)),
                  pl.BlockSpec((seq, HEAD_DIM), lambda b, i: (b, 0)),
                  pl.BlockSpec((seq, LANES), lambda b, i: (b, 0)),
                  pl.BlockSpec((seq // KB, HEAD_DIM, KB), lambda b, i: (b, 0, 0))],
        out_specs=pl.BlockSpec((QB, N_HEADS * HEAD_DIM), lambda b, i: (b * nq + i, 0)),
        scratch_shapes=[pltpu.VMEM((seq, QB), i32),
                        pltpu.VMEM((HEAD_DIM, N_HEADS * QB), f32),
                        pltpu.VMEM((1, N_HEADS * QB), f32),
                        pltpu.VMEM((1, N_HEADS * QB), f32)],
        compiler_params=_cparams(("parallel", "arbitrary")),
        name="attn",
    )(qT, qiT, wT, kr, kir, vT)


CT = 256
HALO = 32


def _conv_kernel(u_ref, w_ref, b_ref, g_ref, beta_ref, c_ref, ybuf_ref, cbuf_ref):
    i = pl.program_id(1)

    @pl.when(i == 0)
    def _():
        ybuf_ref[0:HALO, :] = jnp.zeros((HALO, CONV_WIDTH), f32)

    @pl.when(i > 0)
    def _():
        ybuf_ref[0:HALO, :] = ybuf_ref[CT:CT + HALO, :]

    a = u_ref[:, 0:CONV_WIDTH].astype(f32)
    gate = u_ref[:, CONV_WIDTH:2 * CONV_WIDTH].astype(f32)
    ybuf_ref[HALO:HALO + CT, :] = a * jax.nn.sigmoid(gate)

    off = HALO - (CONV_KERNEL - 1)
    for c in range(CONV_WIDTH // LANES):
        cs = slice(c * LANES, (c + 1) * LANES)
        acc = jnp.broadcast_to(b_ref[:, cs], (CT, LANES))
        for k in range(CONV_KERNEL):
            acc = acc + w_ref[k:k + 1, cs] * ybuf_ref[off + k:off + k + CT, cs]
        cbuf_ref[:, cs] = acc

    y = cbuf_ref[...]
    mu = jnp.mean(y, axis=-1, keepdims=True)
    yc = y - mu
    var = jnp.mean(yc * yc, axis=-1, keepdims=True)
    yn = yc * lax.rsqrt(var + EPS) * g_ref[...] + beta_ref[...]
    c_ref[...] = (yn * jax.nn.sigmoid(yn)).astype(c_ref.dtype)


def _conv(z, dw_w, dw_b, ln_g, ln_b, batch, seq):
    nt = seq // CT
    n = batch * seq
    return pl.pallas_call(
        _conv_kernel,
        out_shape=jax.ShapeDtypeStruct((n, CONV_WIDTH), bf16),
        grid=(batch, nt),
        in_specs=[pl.BlockSpec((CT, 2 * CONV_WIDTH), lambda b, i: (b * nt + i, Z_U // 2048)),
                  pl.BlockSpec((32, CONV_WIDTH), lambda b, i: (0, 0)),
                  pl.BlockSpec((1, CONV_WIDTH), lambda b, i: (0, 0)),
                  pl.BlockSpec((1, CONV_WIDTH), lambda b, i: (0, 0)),
                  pl.BlockSpec((1, CONV_WIDTH), lambda b, i: (0, 0))],
        out_specs=pl.BlockSpec((CT, CONV_WIDTH), lambda b, i: (b * nt + i, 0)),
        scratch_shapes=[pltpu.VMEM((CT + HALO, CONV_WIDTH), f32), pltpu.VMEM((CT, CONV_WIDTH), f32)],
        compiler_params=_cparams(("parallel", "arbitrary")),
        name="conv",
    )(z, dw_w, dw_b, ln_g, ln_b)


MT = 256


def _mix_kernel(c_ref, a_ref, gc_ref, ga_ref, x_ref, wco_ref, wao_ref, wout_ref, g2_ref, wr_ref, br_ref,
                h1_ref, route_ref):
    yc = jnp.dot(c_ref[...], wco_ref[...], preferred_element_type=f32)
    ya = jnp.dot(a_ref[...], wao_ref[...], preferred_element_type=f32)
    mix = jax.nn.sigmoid(gc_ref[...].astype(f32)) * yc + jax.nn.sigmoid(ga_ref[...].astype(f32)) * ya
    h1 = x_ref[...] + jnp.dot(mix.astype(bf16), wout_ref[...], preferred_element_type=f32)
    h1_ref[...] = h1

    ms = jnp.mean(h1 * h1, axis=-1, keepdims=True)
    hn = h1 * lax.rsqrt(ms + EPS) * g2_ref[...]
    lg = jnp.dot(hn, wr_ref[...], preferred_element_type=f32, precision=lax.Precision.HIGHEST) + br_ref[...]
    lane = lax.broadcasted_iota(i32, lg.shape, 1)
    big = jnp.int32(1 << 20)

    gmask = lane < N_GROUPS
    gl = jnp.where(gmask, lg, -jnp.inf)
    gmax = jnp.max(gl, axis=-1, keepdims=True)
    gidx = jnp.min(jnp.where(gl == gmax, lane, big), axis=-1, keepdims=True)
    pg = 1.0 / jnp.sum(jnp.where(gmask, jnp.exp(gl - gmax), 0.0), axis=-1, keepdims=True)

    lo = N_GROUPS + EXPERTS_PER_GROUP * gidx
    emask = (lane >= lo) & (lane < lo + EXPERTS_PER_GROUP)
    el = jnp.where(emask, lg, -jnp.inf)
    m0 = jnp.max(el, axis=-1, keepdims=True)
    i0 = jnp.min(jnp.where(el == m0, lane, big), axis=-1, keepdims=True)
    el1 = jnp.where(lane == i0, -jnp.inf, el)
    m1 = jnp.max(el1, axis=-1, keepdims=True)
    i1 = jnp.min(jnp.where(el1 == m1, lane, big), axis=-1, keepdims=True)
    e1x = jnp.exp(m1 - m0)
    w0 = 1.0 / (1.0 + e1x)
    w1 = e1x * w0

    route = jnp.where(lane == 0, (i0 - N_GROUPS).astype(f32), 0.0)
    route = jnp.where(lane == 1, (i1 - N_GROUPS).astype(f32), route)
    route = jnp.where(lane == 2, pg * w0, route)
    route = jnp.where(lane == 3, pg * w1, route)
    route_ref[...] = route


def _mix(c, a, z, x2, wco, wao, wout, g2, wr, br):
    n = x2.shape[0]
    once = functools.partial(pl.BlockSpec, pipeline_mode=pl.Buffered(1))
    return pl.pallas_call(
        _mix_kernel,
        out_shape=(jax.ShapeDtypeStruct((n, D_MODEL), f32), jax.ShapeDtypeStruct((n, LANES), f32)),
        grid=(n // MT,),
        in_specs=[pl.BlockSpec((MT, CONV_WIDTH), lambda i: (i, 0)),
                  pl.BlockSpec((MT, D_MODEL), lambda i: (i, 0)),
                  pl.BlockSpec((MT, D_MODEL), lambda i: (i, Z_GC // 2048)),
                  pl.BlockSpec((MT, D_MODEL), lambda i: (i, Z_GA // 2048)),
                  pl.BlockSpec((MT, D_MODEL), lambda i: (i, 0)),
                  once((CONV_WIDTH, D_MODEL), lambda i: (0, 0)),
                  once((D_MODEL, D_MODEL), lambda i: (0, 0)),
                  once((D_MODEL, D_MODEL), lambda i: (0, 0)),
                  pl.BlockSpec((1, D_MODEL), lambda i: (0, 0)),
                  once((D_MODEL, LANES), lambda i: (0, 0)),
                  pl.BlockSpec((1, LANES), lambda i: (0, 0))],
        out_specs=(pl.BlockSpec((MT, D_MODEL), lambda i: (i, 0)), pl.BlockSpec((MT, LANES), lambda i: (i, 0))),
        compiler_params=_cparams(("parallel",)),
        name="mix",
    )(c, a, z, z, x2, wco, wao, wout, g2, wr, br)


TR = 256
GC = 2048
CM = 256


def _gather_kernel(idx_ref, src_ref, dst_ref, sem):
    base = pl.program_id(0) * GC

    def body(r, carry):
        pltpu.make_async_copy(src_ref.at[pl.ds(idx_ref[0, r], 1)], dst_ref.at[pl.ds(base + r, 1)], sem).start()
        return carry

    lax.fori_loop(0, GC, body, 0)
    pltpu.make_async_copy(src_ref.at[pl.ds(0, GC)], dst_ref.at[pl.ds(base, GC)], sem).wait()


def _gather_rows(slot_token, src):
    s = slot_token.shape[0]
    idx3 = slot_token.reshape(s // GC, 1, GC)
    return pl.pallas_call(
        _gather_kernel,
        out_shape=jax.ShapeDtypeStruct((s, src.shape[1]), src.dtype),
        grid=(s // GC,),
        in_specs=[pl.BlockSpec((None, 1, GC), lambda i: (i, 0, 0), memory_space=pltpu.SMEM),
                  pl.BlockSpec(memory_space=pl.ANY)],
        out_specs=pl.BlockSpec(memory_space=pl.ANY),
        scratch_shapes=[pltpu.SemaphoreType.DMA(())],
        compiler_params=_cparams(("arbitrary",)),
        name="moe_gather",
    )(idx3, src)


def _expert_kernel(te_ref, nu_ref, xs_ref, g2_ref, wg_ref, wu_ref, wd_ref, o_ref):
    t = pl.program_id(0)

    @pl.when(t < nu_ref[0])
    def _():
        x = xs_ref[...]
        ms = jnp.mean(x * x, axis=-1, keepdims=True)
        hn = (x * lax.rsqrt(ms + EPS) * g2_ref[...]).astype(bf16)
        gate = jnp.dot(hn, wg_ref[...], preferred_element_type=f32)
        up = jnp.dot(hn, wu_ref[...], preferred_element_type=f32)
        act = (gate * jax.nn.sigmoid(gate) * up).astype(bf16)
        o_ref[...] = jnp.dot(act, wd_ref[...], preferred_element_type=f32)

    @pl.when(t >= nu_ref[0])
    def _():
        o_ref[...] = jnp.zeros(o_ref.shape, o_ref.dtype)


def _experts(tile_expert, n_used, xs, g2, wg, wu, wd):
    s = xs.shape[0]
    xmap = lambda t, te, nu: (jnp.minimum(t, nu[0] - 1), 0)
    wmap = lambda t, te, nu: (te[t], 0, 0)
    return pl.pallas_call(
        _expert_kernel,
        out_shape=jax.ShapeDtypeStruct((s, D_MODEL), f32),
        grid_spec=pltpu.PrefetchScalarGridSpec(
            num_scalar_prefetch=2,
            grid=(s // TR,),
            in_specs=[pl.BlockSpec((TR, D_MODEL), xmap),
                      pl.BlockSpec((1, D_MODEL), lambda t, te, nu: (0, 0)),
                      pl.BlockSpec((None, D_MODEL, EXPERT_FF), wmap),
                      pl.BlockSpec((None, D_MODEL, EXPERT_FF), wmap),
                      pl.BlockSpec((None, EXPERT_FF, D_MODEL), wmap)],
            out_specs=pl.BlockSpec((TR, D_MODEL), lambda t, te, nu: (t, 0))),
        compiler_params=_cparams(("arbitrary",)),
        name="moe_experts",
    )(tile_expert, n_used, xs, g2, wg, wu, wd)


def _combine_kernel(pos_ref, h1_ref, route_ref, eo_ref, o_ref, buf_ref, sem):
    def body(r, carry):
        pltpu.make_async_copy(eo_ref.at[pl.ds(pos_ref[0, r], 1)], buf_ref.at[0, pl.ds(r, 1)], sem).start()
        pltpu.make_async_copy(eo_ref.at[pl.ds(pos_ref[0, CM + r], 1)], buf_ref.at[1, pl.ds(r, 1)], sem).start()
        return carry

    lax.fori_loop(0, CM, body, 0)
    pltpu.make_async_copy(eo_ref.at[pl.ds(0, CM)], buf_ref.at[0], sem).wait()
    pltpu.make_async_copy(eo_ref.at[pl.ds(0, CM)], buf_ref.at[1], sem).wait()
    c0 = route_ref[:, 2:3]
    c1 = route_ref[:, 3:4]
    o_ref[...] = h1_ref[...] + c0 * buf_ref[0] + c1 * buf_ref[1]


def _combine(pos01, h1, route, eo):
    n = h1.shape[0]
    return pl.pallas_call(
        _combine_kernel,
        out_shape=jax.ShapeDtypeStruct((n, D_MODEL), f32),
        grid=(n // CM,),
        in_specs=[pl.BlockSpec((None, 1, 2 * CM), lambda i: (i, 0, 0), memory_space=pltpu.SMEM),
                  pl.BlockSpec((CM, D_MODEL), lambda i: (i, 0)),
                  pl.BlockSpec((CM, LANES), lambda i: (i, 0)),
                  pl.BlockSpec(memory_space=pl.ANY)],
        out_specs=pl.BlockSpec((CM, D_MODEL), lambda i: (i, 0)),
        scratch_shapes=[pltpu.VMEM((2, CM, D_MODEL), f32), pltpu.SemaphoreType.DMA(())],
        compiler_params=_cparams(("arbitrary",)),
        name="moe_combine",
    )(pos01, h1, route, eo)


def _route_tables(route, n):
    e = route[:, 0:2].astype(i32).reshape(-1)
    onehot = (e[:, None] == jnp.arange(N_EXPERTS, dtype=i32)[None, :]).astype(i32)
    csum = jnp.cumsum(onehot, axis=0)
    rank = jnp.take_along_axis(csum, e[:, None], axis=1)[:, 0] - 1
    counts = csum[-1]
    ntile = (counts + TR - 1) // TR
    tile_end = jnp.cumsum(ntile)
    tile_start = tile_end - ntile
    pos = tile_start[e] * TR + rank
    s = 2 * n + N_EXPERTS * TR
    slot_token = jnp.zeros((s,), i32).at[pos].set(jnp.arange(2 * n, dtype=i32) // 2)
    n_used = tile_end[-1:]
    t = jnp.arange(s // TR, dtype=i32)
    tile_expert = jnp.searchsorted(tile_end, jnp.minimum(t, n_used[0] - 1), side="right").astype(i32)
    pos01 = jnp.concatenate([pos[0::2].reshape(n // CM, 1, CM), pos[1::2].reshape(n // CM, 1, CM)], axis=2)
    return slot_token, tile_expert, n_used.astype(i32), pos01


def _pack_w_in(w):
    u, q, k, v, qi, ki, wi, gc, ga = jnp.split(w, [2048, 4096, 4224, 4352, 5376, 5440, 5456, 7504], axis=1)
    pad = jnp.zeros((w.shape[0], LANES - IDX_DIM - IDX_HEADS), w.dtype)
    return jnp.concatenate([u, q, gc, ga, qi, k, v, ki, wi, pad], axis=1).astype(bf16)


def _lane_inv(rot_dim, period):
    inv = ROPE_THETA ** (-jnp.arange(0, rot_dim, 2, dtype=f32) / rot_dim)
    lane = jnp.arange(LANES)
    lp = lane % period
    return jnp.where(lp < rot_dim, inv[lp % (rot_dim // 2)], 0.0).astype(f32)[None, :]


def kernel(x, positions, attn_norm_g, w_in, conv_dw_w, conv_dw_b, conv_ln_g, conv_ln_b, w_conv_out, q_norm_g, k_norm_g, w_attn_o, w_out, ffn_norm_g, w_router_group, b_router_group, w_router_expert, b_router_expert, w_exp_gate, w_exp_up, w_exp_down):
    batch, seq, d = x.shape
    n = batch * seq
    h = x.reshape(n, d)
    pos2 = positions.reshape(n, 1)
    inv_a = _lane_inv(HEAD_DIM // ROT_FRACTION, LANES)
    inv_i = _lane_inv(IDX_DIM // ROT_FRACTION, IDX_DIM)
    for l in range(attn_norm_g.shape[0]):
        z = _inproj(h, attn_norm_g[l][None, :], _pack_w_in(w_in[l]))
        qT, qiT, wT, kr, kir, vT = _prep(z, pos2, inv_a, inv_i, q_norm_g[l][None, :], k_norm_g[l][None, :])
        a = _attention(qT, qiT, wT, kr, kir, vT, batch, seq)
        dw = jnp.concatenate([conv_dw_w[l], jnp.zeros((1, CONV_WIDTH), f32)], axis=0)
        c = _conv(z, dw, conv_dw_b[l][None, :], conv_ln_g[l][None, :], conv_ln_b[l][None, :], batch, seq)
        wr = jnp.concatenate([w_router_group[l], w_router_expert[l],
                              jnp.zeros((d, LANES - N_GROUPS - N_EXPERTS), f32)], axis=1)
        br = jnp.concatenate([b_router_group[l], b_router_expert[l],
                              jnp.zeros((LANES - N_GROUPS - N_EXPERTS,), f32)])[None, :]
        g2 = ffn_norm_g[l][None, :]
        h1, route = _mix(c, a, z, h, w_conv_out[l].astype(bf16), w_attn_o[l].astype(bf16), w_out[l].astype(bf16),
                         g2, wr, br)
        slot_token, tile_expert, n_used, pos01 = _route_tables(route, n)
        xs = _gather_rows(slot_token, h1)
        eo = _experts(tile_expert, n_used, xs, g2, w_exp_gate[l].astype(bf16), w_exp_up[l].astype(bf16),
                      w_exp_down[l].astype(bf16))
        h = _combine(pos01, h1, route, eo)
    return h.reshape(batch, seq, d)
```

```python
import functools

import jax
import jax.numpy as jnp
from jax import lax
from jax.experimental import pallas as pl
from jax.experimental.pallas import tpu as pltpu

f32 = jnp.float32
bf16 = jnp.bfloat16
i32 = jnp.int32

D_MODEL = 2048
CONV_WIDTH = 1024
CONV_KERNEL = 31
N_HEADS = 16
HEAD_DIM = 128
ROPE_THETA = 500000.0
ROT_FRACTION = 4
IDX_HEADS = 16
IDX_DIM = 64
TOPK_MAX = 256
IDX_SCALE = IDX_DIM ** -0.5 * IDX_HEADS ** -0.5
ATTN_SCALE = HEAD_DIM ** -0.5
N_GROUPS = 4
EXPERTS_PER_GROUP = 8
N_EXPERTS = 32
EXPERT_FF = 512
EPS = 1e-6

LANES = 128
SUBLANES = 8
NEG = -1e30
LOGIT_LIMIT = 60.0
SEARCH_INTERP_ITERS = 40
SEARCH_MAX_ITERS = SEARCH_INTERP_ITERS + 34

ZW = 9600
Z_U, Z_Q, Z_GC, Z_GA, Z_QI, Z_K, Z_V, Z_KIW = 0, 2048, 4096, 6144, 8192, 9216, 9344, 9472

QB = 128
KB = 512
VMEM_LIMIT = 56 * 1024 * 1024


def _cparams(sem, flags=None):
    return pltpu.CompilerParams(dimension_semantics=sem, vmem_limit_bytes=VMEM_LIMIT, flags=flags)


def _inproj_kernel(x_ref, g_ref, w_ref, z_ref, hn_ref):
    @pl.when(pl.program_id(1) == 0)
    def _():
        x = x_ref[...]
        ms = jnp.mean(x * x, axis=-1, keepdims=True)
        hn_ref[...] = (x * lax.rsqrt(ms + EPS) * g_ref[...]).astype(bf16)

    z_ref[...] = jnp.dot(hn_ref[...], w_ref[...], preferred_element_type=f32).astype(z_ref.dtype)


def _inproj(x2, g, w_packed, tm=512, tn=1920):
    n = x2.shape[0]
    return pl.pallas_call(
        _inproj_kernel,
        out_shape=jax.ShapeDtypeStruct((n, ZW), bf16),
        grid=(n // tm, ZW // tn),
        in_specs=[pl.BlockSpec((tm, D_MODEL), lambda i, j: (i, 0)),
                  pl.BlockSpec((1, D_MODEL), lambda i, j: (0, 0)),
                  pl.BlockSpec((D_MODEL, tn), lambda i, j: (0, j))],
        out_specs=pl.BlockSpec((tm, tn), lambda i, j: (i, j)),
        scratch_shapes=[pltpu.VMEM((tm, D_MODEL), bf16)],
        compiler_params=_cparams(("parallel", "arbitrary")),
        name="inproj",
    )(x2, g, w_packed)


PT = 512


def _prep_kernel(q_ref, qi_ref, k_ref, v_ref, kiw_ref, pos_ref, inva_ref, invi_ref, qg_ref, kg_ref,
                 qT_ref, qiT_ref, wT_ref, kr_ref, kir_ref, vT_ref):
    lane = lax.broadcasted_iota(i32, (PT, LANES), 1)
    pos = pos_ref[...].astype(f32)
    ang_a = pos * inva_ref[...]
    ca, sa = jnp.cos(ang_a), jnp.sin(ang_a)
    s1a = jnp.where(lane < 16, -sa, 0.0)
    s2a = jnp.where(lane >= 16, sa, 0.0)
    ang_i = pos * invi_ref[...]
    ci, si = jnp.cos(ang_i), jnp.sin(ang_i)
    l64 = lane & 63
    s1i = jnp.where(l64 < 8, -si, 0.0)
    s2i = jnp.where(l64 >= 8, si, 0.0)

    def rope_a(x):
        return x * ca + pltpu.roll(x, 112, 1) * s1a + pltpu.roll(x, 16, 1) * s2a

    def rope_i(x):
        return x * ci + pltpu.roll(x, 120, 1) * s1i + pltpu.roll(x, 8, 1) * s2i

    def rms(x, g):
        ms = jnp.mean(x * x, axis=-1, keepdims=True)
        return x * lax.rsqrt(ms + EPS) * g

    for h in range(N_HEADS):
        qh = q_ref[:, h * HEAD_DIM:(h + 1) * HEAD_DIM].astype(f32)
        qh = rope_a(rms(qh, qg_ref[...])) * ATTN_SCALE
        for sb in range(PT // QB):
            qT_ref[sb, :, h * QB:(h + 1) * QB] = qh[sb * QB:(sb + 1) * QB, :].T.astype(bf16)

    for hp in range(IDX_HEADS // 2):
        x = rope_i(qi_ref[:, hp * LANES:(hp + 1) * LANES].astype(f32))
        for sb in range(PT // QB):
            xt = x[sb * QB:(sb + 1) * QB, :].T
            qiT_ref[sb, :, (2 * hp) * QB:(2 * hp + 1) * QB] = xt[0:IDX_DIM].astype(bf16)
            qiT_ref[sb, :, (2 * hp + 1) * QB:(2 * hp + 2) * QB] = xt[IDX_DIM:2 * IDX_DIM].astype(bf16)

    kiw = kiw_ref[...].astype(f32)
    w = pltpu.roll(kiw, 64, 1) * IDX_SCALE
    for sb in range(PT // QB):
        wT_ref[sb] = w[sb * QB:(sb + 1) * QB, :].T[0:IDX_HEADS]

    kir_ref[...] = jnp.where(lane < IDX_DIM, rope_i(kiw), 0.0).astype(bf16)
    kr_ref[...] = rope_a(rms(k_ref[...].astype(f32), kg_ref[...])).astype(bf16)
    vT_ref[0] = v_ref[...].astype(f32).T.astype(bf16)


def _prep(z, pos2, inv_a, inv_i, qg, kg):
    n = z.shape[0]
    nblk = n // QB
    cb = lambda c, w: c // w
    return pl.pallas_call(
        _prep_kernel,
        out_shape=(jax.ShapeDtypeStruct((nblk, HEAD_DIM, N_HEADS * QB), bf16),
                   jax.ShapeDtypeStruct((nblk, IDX_DIM, IDX_HEADS * QB), bf16),
                   jax.ShapeDtypeStruct((nblk, IDX_HEADS, QB), f32),
                   jax.ShapeDtypeStruct((n, HEAD_DIM), bf16),
                   jax.ShapeDtypeStruct((n, LANES), bf16),
                   jax.ShapeDtypeStruct((n // KB, HEAD_DIM, KB), bf16)),
        grid=(n // PT,),
        in_specs=[pl.BlockSpec((PT, 2048), lambda i: (i, cb(Z_Q, 2048))),
                  pl.BlockSpec((PT, 1024), lambda i: (i, cb(Z_QI, 1024))),
                  pl.BlockSpec((PT, LANES), lambda i: (i, cb(Z_K, LANES))),
                  pl.BlockSpec((PT, LANES), lambda i: (i, cb(Z_V, LANES))),
                  pl.BlockSpec((PT, LANES), lambda i: (i, cb(Z_KIW, LANES))),
                  pl.BlockSpec((PT, 1), lambda i: (i, 0)),
                  pl.BlockSpec((1, LANES), lambda i: (0, 0)),
                  pl.BlockSpec((1, LANES), lambda i: (0, 0)),
                  pl.BlockSpec((1, LANES), lambda i: (0, 0)),
                  pl.BlockSpec((1, LANES), lambda i: (0, 0))],
        out_specs=(pl.BlockSpec((PT // QB, HEAD_DIM, N_HEADS * QB), lambda i: (i, 0, 0)),
                   pl.BlockSpec((PT // QB, IDX_DIM, IDX_HEADS * QB), lambda i: (i, 0, 0)),
                   pl.BlockSpec((PT // QB, IDX_HEADS, QB), lambda i: (i, 0, 0)),
                   pl.BlockSpec((PT, HEAD_DIM), lambda i: (i, 0)),
                   pl.BlockSpec((PT, LANES), lambda i: (i, 0)),
                   pl.BlockSpec((1, HEAD_DIM, KB), lambda i: (i, 0, 0))),
        compiler_params=_cparams(("parallel",)),
        name="prep",
    )(z, z, z, z, z, pos2, inv_a, inv_i, qg, kg)


def _f32_to_key(x):
    bits = pltpu.bitcast(x, i32)
    return bits ^ ((bits >> 31) & jnp.int32(0x7FFFFFFF))


def _key_to_f32(k):
    return pltpu.bitcast(k ^ ((k >> 31) & jnp.int32(0x7FFFFFFF)), f32)


def _attn_kernel(qT_ref, qiT_ref, wT_ref, kr_ref, kir_ref, vT_ref, o_ref, sc_ref, acc_ref, m_ref, l_ref, kn_ref):
    qb = pl.program_id(1)
    q0 = qb * QB

    @pl.when(qb == 0)
    def _():
        def kn_body(j, mx):
            kf = kr_ref[pl.ds(pl.multiple_of(j * KB, KB), KB), :].astype(f32)
            return jnp.maximum(mx, jnp.max(jnp.sum(kf * kf, axis=1, keepdims=True), axis=0, keepdims=True))

        kn = lax.fori_loop(0, kr_ref.shape[0] // KB, kn_body, jnp.zeros((1, 1), f32))
        kn_ref[...] = jnp.broadcast_to(kn, kn_ref.shape)

    nkb = (q0 + QB - 1) // KB + 1
    qidx = q0 + lax.broadcasted_iota(i32, (KB, QB), 1)
    row = lax.broadcasted_iota(i32, (KB, QB), 0)

    def score_body(j, carry):
        k0 = pl.multiple_of(j * KB, KB)
        kblk = kir_ref[pl.ds(k0, KB), 0:IDX_DIM]
        sc = jnp.zeros((KB, QB), f32)
        for hp in range(IDX_HEADS // 2):
            s = jnp.dot(kblk, qiT_ref[:, hp * 2 * QB:(hp + 1) * 2 * QB], preferred_element_type=f32)
            s = jnp.maximum(s, 0.0)
            sc = sc + s[:, 0:QB] * wT_ref[2 * hp:2 * hp + 1, :] + s[:, QB:2 * QB] * wT_ref[2 * hp + 1:2 * hp + 2, :]
        causal = k0 + row <= qidx
        sc_ref[pl.ds(k0, KB), :] = jnp.where(causal, sc, -jnp.inf)
        mn, mx = carry
        mn = jnp.minimum(mn, jnp.min(jnp.where(causal, sc, jnp.inf).reshape(KB // 8, 8, QB), axis=0))
        mx = jnp.maximum(mx, jnp.max(jnp.where(causal, sc, -jnp.inf).reshape(KB // 8, 8, QB), axis=0))
        return mn, mx

    mn8, mx8 = lax.fori_loop(0, nkb, score_body,
                             (jnp.full((8, QB), jnp.inf, f32), jnp.full((8, QB), -jnp.inf, f32)))

    lane1 = lax.broadcasted_iota(i32, (1, QB), 1)
    kt = jnp.minimum(TOPK_MAX, q0 + lane1 + 1)

    def count_ge(cand):
        candf = _key_to_f32(cand)

        def cnt_body(j, c):
            k0 = pl.multiple_of(j * KB, KB)
            ge = (sc_ref[pl.ds(k0, KB), :] >= candf).astype(i32)
            return c + jnp.sum(ge.reshape(KB // 8, 8, QB), axis=0)

        c8 = lax.fori_loop(0, nkb, cnt_body, jnp.zeros((8, QB), i32))
        return jnp.sum(c8, axis=0, keepdims=True)

    def active(lo, hi, clo):
        return (clo != kt) & (lo + 1 < hi)

    def search_cond(st):
        it, lo, hi, clo, chi, flo, fhi, side = st
        return (it < SEARCH_MAX_ITERS) & (jnp.sum(active(lo, hi, clo).astype(f32)) > 0.0)

    def search_step(st):
        it, lo, hi, clo, chi, flo, fhi, side = st
        act = active(lo, hi, clo)
        lov = _key_to_f32(lo)
        hiv = _key_to_f32(hi - 1)
        guess = _f32_to_key(lov + (hiv - lov) * (flo / (flo - fhi)))
        mid = (lo >> 1) + (hi >> 1) + (lo & hi & 1)
        cand = jnp.where(it < SEARCH_INTERP_ITERS, guess, mid)
        cand = jnp.minimum(jnp.maximum(cand, lo + 1), hi - 1)
        cnt = count_ge(cand)
        up = act & (cnt >= kt)
        dn = act & (cnt < kt)
        f_new = (cnt - kt).astype(f32)
        flo2 = jnp.where(up, f_new + 0.5, jnp.where(dn & (side < 0), flo * 0.5, flo))
        fhi2 = jnp.where(dn, f_new - 0.5, jnp.where(up & (side > 0), fhi * 0.5, fhi))
        side2 = jnp.where(up, 1, jnp.where(dn, -1, side))
        return (it + 1, jnp.where(up, cand, lo), jnp.where(dn, cand, hi), jnp.where(up, cnt, clo),
                jnp.where(dn, cnt, chi), flo2, fhi2, side2)

    lo0 = _f32_to_key(jnp.min(mn8, axis=0, keepdims=True))
    hi0 = _f32_to_key(jnp.max(mx8, axis=0, keepdims=True)) + 1
    clo0 = q0 + lane1 + 1
    chi0 = jnp.zeros((1, QB), i32)
    st = (jnp.int32(0), lo0, hi0, clo0, chi0, (clo0 - kt).astype(f32) + 0.5, (chi0 - kt).astype(f32) - 0.5,
          jnp.zeros((1, QB), i32))
    thr = lax.while_loop(search_cond, lambda s: search_step(search_step(s)), st)[1]
    thr_f = jnp.where(clo0 <= TOPK_MAX, -jnp.inf, _key_to_f32(thr))

    l_ref[...] = jnp.zeros(l_ref.shape, f32)
    acc_ref[...] = jnp.zeros(acc_ref.shape, f32)

    def block_bias(j):
        k0 = pl.multiple_of(j * KB, KB)
        sel = (sc_ref[pl.ds(k0, KB), :] >= thr_f) & (k0 + row <= qidx)
        bias1 = jnp.where(sel, 0.0, NEG).astype(f32)
        return k0, jnp.concatenate([bias1] * N_HEADS, axis=1)

    qf = qT_ref[...].astype(f32)
    qn2 = jnp.max(jnp.sum(qf * qf, axis=0, keepdims=True), axis=1, keepdims=True)
    bounded = jnp.max(qn2 * kn_ref[0:1, 0:1]) <= LOGIT_LIMIT * LOGIT_LIMIT

    def att_bounded(j, carry):
        k0, bias = block_bias(j)
        kblk = kr_ref[pl.ds(k0, KB), :]
        vt = vT_ref[j]
        p = jnp.exp(jnp.dot(kblk, qT_ref[...], preferred_element_type=f32) + bias)
        l_ref[...] += jnp.sum(p, axis=0, keepdims=True)
        acc_ref[...] += jnp.dot(vt, p.astype(bf16), preferred_element_type=f32)
        return carry

    def att_online(j, carry):
        k0, bias = block_bias(j)
        kblk = kr_ref[pl.ds(k0, KB), :]
        vt = vT_ref[j]
        s = jnp.dot(kblk, qT_ref[...], preferred_element_type=f32) + bias
        m_old = m_ref[...]
        m_new = jnp.maximum(m_old, jnp.max(s, axis=0, keepdims=True))
        p = jnp.exp(s - m_new)
        alpha = jnp.exp(m_old - m_new)
        l_ref[...] = l_ref[...] * alpha + jnp.sum(p, axis=0, keepdims=True)
        m_ref[...] = m_new
        acc_ref[...] = acc_ref[...] * alpha + jnp.dot(vt, p.astype(bf16), preferred_element_type=f32)
        return carry

    @pl.when(bounded)
    def _():
        lax.fori_loop(0, nkb, att_bounded, 0)

    @pl.when(jnp.logical_not(bounded))
    def _():
        m_ref[...] = jnp.full(m_ref.shape, NEG, f32)
        lax.fori_loop(0, nkb, att_online, 0)

    out_t = acc_ref[...] / l_ref[...]
    for h in range(N_HEADS):
        o_ref[:, h * HEAD_DIM:(h + 1) * HEAD_DIM] = out_t[:, h * QB:(h + 1) * QB].T.astype(o_ref.dtype)


def _attention(qT, qiT, wT, kr, kir, vT, batch, seq):
    nq = seq // QB
    n = batch * seq
    return pl.pallas_call(
        _attn_kernel,
        out_shape=jax.ShapeDtypeStruct((n, N_HEADS * HEAD_DIM), bf16),
        grid=(batch, nq),
        in_specs=[pl.BlockSpec((None, HEAD_DIM, N_HEADS * QB), lambda b, i: (b * nq + i, 0, 0)),
                  pl.BlockSpec((None, IDX_DIM, IDX_HEADS * QB), lambda b, i: (b * nq + i, 0, 0)),
                  pl.BlockSpec((None, IDX_HEADS, QB), lambda b, i: (b * nq + i, 0, 0)),
                  pl.BlockSpec((seq, HEAD_DIM), lambda b, i: (b, 0)),
                  pl.BlockSpec((seq, LANES), lambda b, i: (b, 0)),
                  pl.BlockSpec((seq // KB, HEAD_DIM, KB), lambda b, i: (b, 0, 0))],
        out_specs=pl.BlockSpec((QB, N_HEADS * HEAD_DIM), lambda b, i: (b * nq + i, 0)),
        scratch_shapes=[pltpu.VMEM((seq, QB), f32),
                        pltpu.VMEM((HEAD_DIM, N_HEADS * QB), f32),
                        pltpu.VMEM((1, N_HEADS * QB), f32),
                        pltpu.VMEM((1, N_HEADS * QB), f32),
                        pltpu.VMEM((8, LANES), f32)],
        compiler_params=_cparams(("parallel", "arbitrary")),
        name="attn",
    )(qT, qiT, wT, kr, kir, vT)


CT = 256
HALO = 32


def _conv_kernel(u_ref, w_ref, b_ref, g_ref, beta_ref, c_ref, ybuf_ref, cbuf_ref):
    i = pl.program_id(1)

    @pl.when(i == 0)
    def _():
        ybuf_ref[0:HALO, :] = jnp.zeros((HALO, CONV_WIDTH), f32)

    @pl.when(i > 0)
    def _():
        ybuf_ref[0:HALO, :] = ybuf_ref[CT:CT + HALO, :]

    a = u_ref[:, 0:CONV_WIDTH].astype(f32)
    gate = u_ref[:, CONV_WIDTH:2 * CONV_WIDTH].astype(f32)
    ybuf_ref[HALO:HALO + CT, :] = a * jax.nn.sigmoid(gate)

    off = HALO - (CONV_KERNEL - 1)
    for c in range(CONV_WIDTH // LANES):
        cs = slice(c * LANES, (c + 1) * LANES)
        acc = jnp.broadcast_to(b_ref[:, cs], (CT, LANES))
        for ph in range(SUBLANES):
            rows = CT if ph == 0 else CT + SUBLANES
            part = None
            for o in range(ph, off + CONV_KERNEL, SUBLANES):
                if o < off:
                    continue
                term = w_ref[o - off:o - off + 1, cs] * ybuf_ref[o - ph:o - ph + rows, cs]
                part = term if part is None else part + term
            acc = acc + part[ph:ph + CT]
        cbuf_ref[:, cs] = acc

    y = cbuf_ref[...]
    mu = jnp.mean(y, axis=-1, keepdims=True)
    yc = y - mu
    var = jnp.mean(yc * yc, axis=-1, keepdims=True)
    yn = yc * lax.rsqrt(var + EPS) * g_ref[...] + beta_ref[...]
    c_ref[...] = (yn * jax.nn.sigmoid(yn)).astype(c_ref.dtype)


def _conv(z, dw_w, dw_b, ln_g, ln_b, batch, seq):
    nt = seq // CT
    n = batch * seq
    return pl.pallas_call(
        _conv_kernel,
        out_shape=jax.ShapeDtypeStruct((n, CONV_WIDTH), bf16),
        grid=(batch, nt),
        in_specs=[pl.BlockSpec((CT, 2 * CONV_WIDTH), lambda b, i: (b * nt + i, Z_U // 2048)),
                  pl.BlockSpec((32, CONV_WIDTH), lambda b, i: (0, 0)),
                  pl.BlockSpec((1, CONV_WIDTH), lambda b, i: (0, 0)),
                  pl.BlockSpec((1, CONV_WIDTH), lambda b, i: (0, 0)),
                  pl.BlockSpec((1, CONV_WIDTH), lambda b, i: (0, 0))],
        out_specs=pl.BlockSpec((CT, CONV_WIDTH), lambda b, i: (b * nt + i, 0)),
        scratch_shapes=[pltpu.VMEM((CT + HALO, CONV_WIDTH), f32), pltpu.VMEM((CT, CONV_WIDTH), f32)],
        compiler_params=_cparams(("parallel", "arbitrary")),
        name="conv",
    )(z, dw_w, dw_b, ln_g, ln_b)


MT = 256


def _mix_kernel(c_ref, a_ref, gc_ref, ga_ref, x_ref, wco_ref, wao_ref, wout_ref, g2_ref, wr_ref, br_ref,
                h1_ref, route_ref):
    yc = jnp.dot(c_ref[...], wco_ref[...], preferred_element_type=f32)
    ya = jnp.dot(a_ref[...], wao_ref[...], preferred_element_type=f32)
    mix = jax.nn.sigmoid(gc_ref[...].astype(f32)) * yc + jax.nn.sigmoid(ga_ref[...].astype(f32)) * ya
    h1 = x_ref[...] + jnp.dot(mix.astype(bf16), wout_ref[...], preferred_element_type=f32)
    h1_ref[...] = h1

    ms = jnp.mean(h1 * h1, axis=-1, keepdims=True)
    hn = h1 * lax.rsqrt(ms + EPS) * g2_ref[...]
    lg = jnp.dot(hn, wr_ref[...], preferred_element_type=f32, precision=lax.Precision.HIGHEST) + br_ref[...]
    lane = lax.broadcasted_iota(i32, lg.shape, 1)
    big = jnp.int32(1 << 20)

    gmask = lane < N_GROUPS
    gl = jnp.where(gmask, lg, -jnp.inf)
    gmax = jnp.max(gl, axis=-1, keepdims=True)
    gidx = jnp.min(jnp.where(gl == gmax, lane, big), axis=-1, keepdims=True)
    pg = 1.0 / jnp.sum(jnp.where(gmask, jnp.exp(gl - gmax), 0.0), axis=-1, keepdims=True)

    lo = N_GROUPS + EXPERTS_PER_GROUP * gidx
    emask = (lane >= lo) & (lane < lo + EXPERTS_PER_GROUP)
    el = jnp.where(emask, lg, -jnp.inf)
    m0 = jnp.max(el, axis=-1, keepdims=True)
    i0 = jnp.min(jnp.where(el == m0, lane, big), axis=-1, keepdims=True)
    el1 = jnp.where(lane == i0, -jnp.inf, el)
    m1 = jnp.max(el1, axis=-1, keepdims=True)
    i1 = jnp.min(jnp.where(el1 == m1, lane, big), axis=-1, keepdims=True)
    e1x = jnp.exp(m1 - m0)
    w0 = 1.0 / (1.0 + e1x)
    w1 = e1x * w0

    route = jnp.where(lane == 0, (i0 - N_GROUPS).astype(f32), 0.0)
    route = jnp.where(lane == 1, (i1 - N_GROUPS).astype(f32), route)
    route = jnp.where(lane == 2, pg * w0, route)
    route = jnp.where(lane == 3, pg * w1, route)
    route_ref[...] = route


def _mix(c, a, z, x2, wco, wao, wout, g2, wr, br):
    n = x2.shape[0]
    once = functools.partial(pl.BlockSpec, pipeline_mode=pl.Buffered(1))
    return pl.pallas_call(
        _mix_kernel,
        out_shape=(jax.ShapeDtypeStruct((n, D_MODEL), f32), jax.ShapeDtypeStruct((n, LANES), f32)),
        grid=(n // MT,),
        in_specs=[pl.BlockSpec((MT, CONV_WIDTH), lambda i: (i, 0)),
                  pl.BlockSpec((MT, D_MODEL), lambda i: (i, 0)),
                  pl.BlockSpec((MT, D_MODEL), lambda i: (i, Z_GC // 2048)),
                  pl.BlockSpec((MT, D_MODEL), lambda i: (i, Z_GA // 2048)),
                  pl.BlockSpec((MT, D_MODEL), lambda i: (i, 0)),
                  once((CONV_WIDTH, D_MODEL), lambda i: (0, 0)),
                  once((D_MODEL, D_MODEL), lambda i: (0, 0)),
                  once((D_MODEL, D_MODEL), lambda i: (0, 0)),
                  pl.BlockSpec((1, D_MODEL), lambda i: (0, 0)),
                  once((D_MODEL, LANES), lambda i: (0, 0)),
                  pl.BlockSpec((1, LANES), lambda i: (0, 0))],
        out_specs=(pl.BlockSpec((MT, D_MODEL), lambda i: (i, 0)), pl.BlockSpec((MT, LANES), lambda i: (i, 0))),
        compiler_params=_cparams(("parallel",)),
        name="mix",
    )(c, a, z, z, x2, wco, wao, wout, g2, wr, br)


TR = 256
CM = 256


def _expert_kernel(te_ref, nu_ref, cur_ref, nxt_ref, h_ref, g2_ref, wg_ref, wu_ref, wd_ref, o_ref,
                   xbuf_ref, wgb_ref, wub_ref, wdb_ref, sem):
    t = pl.program_id(0)
    nu = nu_ref[0]
    slot = lax.rem(t, 2)

    def issue(idx_ref, dst_slot):
        def body(r, carry):
            pltpu.make_async_copy(h_ref.at[pl.ds(idx_ref[0, r], 1)], xbuf_ref.at[dst_slot, pl.ds(r, 1)],
                                  sem.at[dst_slot]).start()
            return carry

        lax.fori_loop(0, TR, body, 0, unroll=8)

    @pl.when(t == 0)
    def _():
        issue(cur_ref, 0)

    @pl.when(t + 1 < nu)
    def _():
        issue(nxt_ref, 1 - slot)

    @pl.when(t < nu)
    def _():
        @pl.when((t == 0) | (te_ref[t] != te_ref[jnp.maximum(t - 1, 0)]))
        def _():
            wgb_ref[...] = wg_ref[...].astype(bf16)
            wub_ref[...] = wu_ref[...].astype(bf16)
            wdb_ref[...] = wd_ref[...].astype(bf16)

        pltpu.make_async_copy(h_ref.at[pl.ds(0, TR)], xbuf_ref.at[slot], sem.at[slot]).wait()
        x = xbuf_ref[slot]
        ms = jnp.mean(x * x, axis=-1, keepdims=True)
        hn = (x * lax.rsqrt(ms + EPS) * g2_ref[...]).astype(bf16)
        gate = jnp.dot(hn, wgb_ref[...], preferred_element_type=f32)
        up = jnp.dot(hn, wub_ref[...], preferred_element_type=f32)
        act = (gate * jax.nn.sigmoid(gate) * up).astype(bf16)
        o_ref[...] = jnp.dot(act, wdb_ref[...], preferred_element_type=f32)

    @pl.when(t >= nu)
    def _():
        o_ref[...] = jnp.zeros(o_ref.shape, o_ref.dtype)


def _experts(tile_expert, n_used, slot_token, h1, g2, wg, wu, wd):
    s = slot_token.shape[0]
    ntiles = s // TR
    idx3 = slot_token.reshape(ntiles, 1, TR)
    wmap = lambda t, te, nu: (te[t], 0, 0)
    return pl.pallas_call(
        _expert_kernel,
        out_shape=jax.ShapeDtypeStruct((s, D_MODEL), f32),
        grid_spec=pltpu.PrefetchScalarGridSpec(
            num_scalar_prefetch=2,
            grid=(ntiles,),
            in_specs=[pl.BlockSpec((None, 1, TR), lambda t, te, nu: (t, 0, 0), memory_space=pltpu.SMEM),
                      pl.BlockSpec((None, 1, TR), lambda t, te, nu: (jnp.minimum(t + 1, ntiles - 1), 0, 0),
                                   memory_space=pltpu.SMEM),
                      pl.BlockSpec(memory_space=pl.ANY),
                      pl.BlockSpec((1, D_MODEL), lambda t, te, nu: (0, 0)),
                      pl.BlockSpec((None, D_MODEL, EXPERT_FF), wmap),
                      pl.BlockSpec((None, D_MODEL, EXPERT_FF), wmap),
                      pl.BlockSpec((None, EXPERT_FF, D_MODEL), wmap)],
            out_specs=pl.BlockSpec((TR, D_MODEL), lambda t, te, nu: (t, 0)),
            scratch_shapes=[pltpu.VMEM((2, TR, D_MODEL), f32),
                            pltpu.VMEM((D_MODEL, EXPERT_FF), bf16),
                            pltpu.VMEM((D_MODEL, EXPERT_FF), bf16),
                            pltpu.VMEM((EXPERT_FF, D_MODEL), bf16),
                            pltpu.SemaphoreType.DMA((2,))]),
        compiler_params=_cparams(("arbitrary",)),
        name="moe_experts",
    )(tile_expert, n_used, idx3, idx3, h1, g2, wg, wu, wd)


def _combine_kernel(cur_ref, nxt_ref, h1_ref, route_ref, eo_ref, o_ref, buf_ref, sem):
    i = pl.program_id(0)
    slot = lax.rem(i, 2)

    def issue(idx_ref, dst_slot):
        def body(r, carry):
            pltpu.make_async_copy(eo_ref.at[pl.ds(idx_ref[0, r], 1)], buf_ref.at[dst_slot, 0, pl.ds(r, 1)],
                                  sem.at[dst_slot]).start()
            pltpu.make_async_copy(eo_ref.at[pl.ds(idx_ref[0, CM + r], 1)], buf_ref.at[dst_slot, 1, pl.ds(r, 1)],
                                  sem.at[dst_slot]).start()
            return carry

        lax.fori_loop(0, CM, body, 0, unroll=8)

    @pl.when(i == 0)
    def _():
        issue(cur_ref, 0)

    @pl.when(i + 1 < pl.num_programs(0))
    def _():
        issue(nxt_ref, 1 - slot)

    pltpu.make_async_copy(eo_ref.at[pl.ds(0, CM)], buf_ref.at[slot, 0], sem.at[slot]).wait()
    pltpu.make_async_copy(eo_ref.at[pl.ds(0, CM)], buf_ref.at[slot, 1], sem.at[slot]).wait()
    c0 = route_ref[:, 2:3]
    c1 = route_ref[:, 3:4]
    o_ref[...] = h1_ref[...] + c0 * buf_ref[slot, 0] + c1 * buf_ref[slot, 1]


def _combine(pos01, h1, route, eo):
    n = h1.shape[0]
    nt = n // CM
    return pl.pallas_call(
        _combine_kernel,
        out_shape=jax.ShapeDtypeStruct((n, D_MODEL), f32),
        grid=(nt,),
        in_specs=[pl.BlockSpec((None, 1, 2 * CM), lambda i: (i, 0, 0), memory_space=pltpu.SMEM),
                  pl.BlockSpec((None, 1, 2 * CM), lambda i: (jnp.minimum(i + 1, nt - 1), 0, 0),
                               memory_space=pltpu.SMEM),
                  pl.BlockSpec((CM, D_MODEL), lambda i: (i, 0)),
                  pl.BlockSpec((CM, LANES), lambda i: (i, 0)),
                  pl.BlockSpec(memory_space=pl.ANY)],
        out_specs=pl.BlockSpec((CM, D_MODEL), lambda i: (i, 0)),
        scratch_shapes=[pltpu.VMEM((2, 2, CM, D_MODEL), f32), pltpu.SemaphoreType.DMA((2,))],
        compiler_params=_cparams(("arbitrary",)),
        name="moe_combine",
    )(pos01, pos01, h1, route, eo)


def _route_tables(route, n):
    e = route[:, 0:2].astype(i32).reshape(-1)
    onehot = (e[:, None] == jnp.arange(N_EXPERTS, dtype=i32)[None, :]).astype(i32)
    csum = jnp.cumsum(onehot, axis=0)
    rank = jnp.take_along_axis(csum, e[:, None], axis=1)[:, 0] - 1
    counts = csum[-1]
    ntile = (counts + TR - 1) // TR
    tile_end = jnp.cumsum(ntile)
    tile_start = tile_end - ntile
    pos = tile_start[e] * TR + rank
    s = 2 * n + N_EXPERTS * TR
    slot_token = jnp.zeros((s,), i32).at[pos].set(jnp.arange(2 * n, dtype=i32) // 2)
    n_used = tile_end[-1:]
    t = jnp.minimum(jnp.arange(s // TR, dtype=i32), n_used[0] - 1)
    tile_expert = jnp.sum((tile_end[None, :] <= t[:, None]).astype(i32), axis=1)
    pos01 = jnp.concatenate([pos[0::2].reshape(n // CM, 1, CM), pos[1::2].reshape(n // CM, 1, CM)], axis=2)
    return slot_token, tile_expert, n_used.astype(i32), pos01


def _pack_w_in(w):
    u, q, k, v, qi, ki, wi, gc, ga = jnp.split(w, [2048, 4096, 4224, 4352, 5376, 5440, 5456, 7504], axis=1)
    pad = jnp.zeros((w.shape[0], LANES - IDX_DIM - IDX_HEADS), w.dtype)
    return jnp.concatenate([u, q, gc, ga, qi, k, v, ki, wi, pad], axis=1).astype(bf16)


def _lane_inv(rot_dim, period):
    inv = ROPE_THETA ** (-jnp.arange(0, rot_dim, 2, dtype=f32) / rot_dim)
    lane = jnp.arange(LANES)
    lp = lane % period
    return jnp.where(lp < rot_dim, inv[lp % (rot_dim // 2)], 0.0).astype(f32)[None, :]


def kernel(x, positions, attn_norm_g, w_in, conv_dw_w, conv_dw_b, conv_ln_g, conv_ln_b, w_conv_out, q_norm_g, k_norm_g, w_attn_o, w_out, ffn_norm_g, w_router_group, b_router_group, w_router_expert, b_router_expert, w_exp_gate, w_exp_up, w_exp_down):
    batch, seq, d = x.shape
    n = batch * seq
    h = x.reshape(n, d)
    pos2 = positions.reshape(n, 1)
    inv_a = _lane_inv(HEAD_DIM // ROT_FRACTION, LANES)
    inv_i = _lane_inv(IDX_DIM // ROT_FRACTION, IDX_DIM)
    for l in range(attn_norm_g.shape[0]):
        z = _inproj(h, attn_norm_g[l][None, :], _pack_w_in(w_in[l]))
        qT, qiT, wT, kr, kir, vT = _prep(z, pos2, inv_a, inv_i, q_norm_g[l][None, :], k_norm_g[l][None, :])
        a = _attention(qT, qiT, wT, kr, kir, vT, batch, seq)
        dw = jnp.concatenate([conv_dw_w[l], jnp.zeros((1, CONV_WIDTH), f32)], axis=0)
        c = _conv(z, dw, conv_dw_b[l][None, :], conv_ln_g[l][None, :], conv_ln_b[l][None, :], batch, seq)
        wr = jnp.concatenate([w_router_group[l], w_router_expert[l],
                              jnp.zeros((d, LANES - N_GROUPS - N_EXPERTS), f32)], axis=1)
        br = jnp.concatenate([b_router_group[l], b_router_expert[l],
                              jnp.zeros((LANES - N_GROUPS - N_EXPERTS,), f32)])[None, :]
        g2 = ffn_norm_g[l][None, :]
        h1, route = _mix(c, a, z, h, w_conv_out[l].astype(bf16), w_attn_o[l].astype(bf16), w_out[l].astype(bf16),
                         g2, wr, br)
        slot_token, tile_expert, n_used, pos01 = _route_tables(route, n)
        eo = _experts(tile_expert, n_used, slot_token, h1, g2, w_exp_gate[l], w_exp_up[l], w_exp_down[l])
        h = _combine(pos01, h1, route, eo)
    return h.reshape(batch, seq, d)
```

```python
import functools

import jax
import jax.numpy as jnp
from jax import lax
from jax.experimental import pallas as pl
from jax.experimental.pallas import tpu as pltpu

f32 = jnp.float32
bf16 = jnp.bfloat16
i32 = jnp.int32

D_MODEL = 2048
CONV_WIDTH = 1024
CONV_KERNEL = 31
N_HEADS = 16
HEAD_DIM = 128
ROPE_THETA = 500000.0
ROT_FRACTION = 4
IDX_HEADS = 16
IDX_DIM = 64
TOPK_MAX = 256
IDX_SCALE = IDX_DIM ** -0.5 * IDX_HEADS ** -0.5
ATTN_SCALE = HEAD_DIM ** -0.5
N_GROUPS = 4
EXPERTS_PER_GROUP = 8
N_EXPERTS = 32
EXPERT_FF = 512
EPS = 1e-6

LANES = 128
SUBLANES = 8
NEG = -1e30
LOGIT_LIMIT = 60.0
SEARCH_INTERP_ITERS = 40
SEARCH_MAX_ITERS = SEARCH_INTERP_ITERS + 34

ZW = 9600
Z_U, Z_Q, Z_GC, Z_GA, Z_QI, Z_K, Z_V, Z_KIW = 0, 2048, 4096, 6144, 8192, 9216, 9344, 9472

QB = 128
KB = 512
VMEM_LIMIT = 56 * 1024 * 1024


def _cparams(sem, flags=None):
    return pltpu.CompilerParams(dimension_semantics=sem, vmem_limit_bytes=VMEM_LIMIT, flags=flags)


def _inproj_kernel(x_ref, g_ref, w_ref, z_ref, hn_ref):
    @pl.when(pl.program_id(1) == 0)
    def _():
        x = x_ref[...]
        ms = jnp.mean(x * x, axis=-1, keepdims=True)
        hn_ref[...] = (x * lax.rsqrt(ms + EPS) * g_ref[...]).astype(bf16)

    z_ref[...] = jnp.dot(hn_ref[...], w_ref[...], preferred_element_type=f32).astype(z_ref.dtype)


def _inproj(x2, g, w_packed, tm=512, tn=1920):
    n = x2.shape[0]
    return pl.pallas_call(
        _inproj_kernel,
        out_shape=jax.ShapeDtypeStruct((n, ZW), bf16),
        grid=(n // tm, ZW // tn),
        in_specs=[pl.BlockSpec((tm, D_MODEL), lambda i, j: (i, 0)),
                  pl.BlockSpec((1, D_MODEL), lambda i, j: (0, 0)),
                  pl.BlockSpec((D_MODEL, tn), lambda i, j: (0, j))],
        out_specs=pl.BlockSpec((tm, tn), lambda i, j: (i, j)),
        scratch_shapes=[pltpu.VMEM((tm, D_MODEL), bf16)],
        compiler_params=_cparams(("parallel", "arbitrary")),
        name="inproj",
    )(x2, g, w_packed)


PT = 512


def _prep_kernel(q_ref, qi_ref, k_ref, v_ref, kiw_ref, pos_ref, inva_ref, invi_ref, qg_ref, kg_ref,
                 qT_ref, qiT_ref, wT_ref, kr_ref, kir_ref, vT_ref):
    lane = lax.broadcasted_iota(i32, (PT, LANES), 1)
    pos = pos_ref[...].astype(f32)
    ang_a = pos * inva_ref[...]
    ca, sa = jnp.cos(ang_a), jnp.sin(ang_a)
    s1a = jnp.where(lane < 16, -sa, 0.0)
    s2a = jnp.where(lane >= 16, sa, 0.0)
    ang_i = pos * invi_ref[...]
    ci, si = jnp.cos(ang_i), jnp.sin(ang_i)
    l64 = lane & 63
    s1i = jnp.where(l64 < 8, -si, 0.0)
    s2i = jnp.where(l64 >= 8, si, 0.0)

    def rope_a(x):
        return x * ca + pltpu.roll(x, 112, 1) * s1a + pltpu.roll(x, 16, 1) * s2a

    def rope_i(x):
        return x * ci + pltpu.roll(x, 120, 1) * s1i + pltpu.roll(x, 8, 1) * s2i

    def rms(x, g):
        ms = jnp.mean(x * x, axis=-1, keepdims=True)
        return x * lax.rsqrt(ms + EPS) * g

    for h in range(N_HEADS):
        qh = q_ref[:, h * HEAD_DIM:(h + 1) * HEAD_DIM].astype(f32)
        qh = rope_a(rms(qh, qg_ref[...])) * ATTN_SCALE
        for sb in range(PT // QB):
            qT_ref[sb, :, h * QB:(h + 1) * QB] = qh[sb * QB:(sb + 1) * QB, :].T.astype(bf16)

    for hp in range(IDX_HEADS // 2):
        x = rope_i(qi_ref[:, hp * LANES:(hp + 1) * LANES].astype(f32))
        for sb in range(PT // QB):
            xt = x[sb * QB:(sb + 1) * QB, :].T
            qiT_ref[sb, :, (2 * hp) * QB:(2 * hp + 1) * QB] = xt[0:IDX_DIM].astype(bf16)
            qiT_ref[sb, :, (2 * hp + 1) * QB:(2 * hp + 2) * QB] = xt[IDX_DIM:2 * IDX_DIM].astype(bf16)

    kiw = kiw_ref[...].astype(f32)
    w = pltpu.roll(kiw, 64, 1) * IDX_SCALE
    for sb in range(PT // QB):
        wT_ref[sb] = w[sb * QB:(sb + 1) * QB, :].T[0:IDX_HEADS]

    kir_ref[...] = jnp.where(lane < IDX_DIM, rope_i(kiw), 0.0).astype(bf16)
    kr_ref[...] = rope_a(rms(k_ref[...].astype(f32), kg_ref[...])).astype(bf16)
    vT_ref[0] = v_ref[...].astype(f32).T.astype(bf16)


def _prep(z, pos2, inv_a, inv_i, qg, kg):
    n = z.shape[0]
    nblk = n // QB
    cb = lambda c, w: c // w
    return pl.pallas_call(
        _prep_kernel,
        out_shape=(jax.ShapeDtypeStruct((nblk, HEAD_DIM, N_HEADS * QB), bf16),
                   jax.ShapeDtypeStruct((nblk, IDX_DIM, IDX_HEADS * QB), bf16),
                   jax.ShapeDtypeStruct((nblk, IDX_HEADS, QB), f32),
                   jax.ShapeDtypeStruct((n, HEAD_DIM), bf16),
                   jax.ShapeDtypeStruct((n, LANES), bf16),
                   jax.ShapeDtypeStruct((n // KB, HEAD_DIM, KB), bf16)),
        grid=(n // PT,),
        in_specs=[pl.BlockSpec((PT, 2048), lambda i: (i, cb(Z_Q, 2048))),
                  pl.BlockSpec((PT, 1024), lambda i: (i, cb(Z_QI, 1024))),
                  pl.BlockSpec((PT, LANES), lambda i: (i, cb(Z_K, LANES))),
                  pl.BlockSpec((PT, LANES), lambda i: (i, cb(Z_V, LANES))),
                  pl.BlockSpec((PT, LANES), lambda i: (i, cb(Z_KIW, LANES))),
                  pl.BlockSpec((PT, 1), lambda i: (i, 0)),
                  pl.BlockSpec((1, LANES), lambda i: (0, 0)),
                  pl.BlockSpec((1, LANES), lambda i: (0, 0)),
                  pl.BlockSpec((1, LANES), lambda i: (0, 0)),
                  pl.BlockSpec((1, LANES), lambda i: (0, 0))],
        out_specs=(pl.BlockSpec((PT // QB, HEAD_DIM, N_HEADS * QB), lambda i: (i, 0, 0)),
                   pl.BlockSpec((PT // QB, IDX_DIM, IDX_HEADS * QB), lambda i: (i, 0, 0)),
                   pl.BlockSpec((PT // QB, IDX_HEADS, QB), lambda i: (i, 0, 0)),
                   pl.BlockSpec((PT, HEAD_DIM), lambda i: (i, 0)),
                   pl.BlockSpec((PT, LANES), lambda i: (i, 0)),
                   pl.BlockSpec((1, HEAD_DIM, KB), lambda i: (i, 0, 0))),
        compiler_params=_cparams(("parallel",)),
        name="prep",
    )(z, z, z, z, z, pos2, inv_a, inv_i, qg, kg)


def _f32_to_key(x):
    bits = pltpu.bitcast(x, i32)
    return bits ^ ((bits >> 31) & jnp.int32(0x7FFFFFFF))


def _key_to_f32(k):
    return pltpu.bitcast(k ^ ((k >> 31) & jnp.int32(0x7FFFFFFF)), f32)


def _attn_kernel(qT_ref, qiT_ref, wT_ref, kr_ref, kir_ref, vT_ref, o_ref, sc_ref, acc_ref, m_ref, l_ref, kn_ref):
    qb = pl.program_id(1)
    q0 = qb * QB

    @pl.when(qb == 0)
    def _():
        def kn_body(j, mx):
            kf = kr_ref[pl.ds(pl.multiple_of(j * KB, KB), KB), :].astype(f32)
            return jnp.maximum(mx, jnp.max(jnp.sum(kf * kf, axis=1, keepdims=True), axis=0, keepdims=True))

        kn = lax.fori_loop(0, kr_ref.shape[0] // KB, kn_body, jnp.zeros((1, 1), f32))
        kn_ref[...] = jnp.broadcast_to(kn, kn_ref.shape)

    nkb = (q0 + QB - 1) // KB + 1
    qidx = q0 + lax.broadcasted_iota(i32, (KB, QB), 1)
    row = lax.broadcasted_iota(i32, (KB, QB), 0)

    def score_body(j, carry):
        k0 = pl.multiple_of(j * KB, KB)
        kblk = kir_ref[pl.ds(k0, KB), 0:IDX_DIM]
        sc = jnp.zeros((KB, QB), f32)
        for hp in range(IDX_HEADS // 2):
            s = jnp.dot(kblk, qiT_ref[:, hp * 2 * QB:(hp + 1) * 2 * QB], preferred_element_type=f32)
            s = jnp.maximum(s, 0.0)
            sc = sc + s[:, 0:QB] * wT_ref[2 * hp:2 * hp + 1, :] + s[:, QB:2 * QB] * wT_ref[2 * hp + 1:2 * hp + 2, :]
        causal = k0 + row <= qidx
        sc_ref[pl.ds(k0, KB), :] = jnp.where(causal, sc, -jnp.inf)
        mn, mx = carry
        mn = jnp.minimum(mn, jnp.min(jnp.where(causal, sc, jnp.inf).reshape(KB // 8, 8, QB), axis=0))
        mx = jnp.maximum(mx, jnp.max(jnp.where(causal, sc, -jnp.inf).reshape(KB // 8, 8, QB), axis=0))
        return mn, mx

    mn8, mx8 = lax.fori_loop(0, nkb, score_body,
                             (jnp.full((8, QB), jnp.inf, f32), jnp.full((8, QB), -jnp.inf, f32)))

    lane1 = lax.broadcasted_iota(i32, (1, QB), 1)
    kt = jnp.minimum(TOPK_MAX, q0 + lane1 + 1)

    def count_ge(cand):
        candf = _key_to_f32(cand)

        def cnt_body(j, c):
            k0 = pl.multiple_of(j * KB, KB)
            ge = (sc_ref[pl.ds(k0, KB), :] >= candf).astype(i32)
            return c + jnp.sum(ge.reshape(KB // 8, 8, QB), axis=0)

        c8 = lax.fori_loop(0, nkb, cnt_body, jnp.zeros((8, QB), i32))
        return jnp.sum(c8, axis=0, keepdims=True)

    def active(lo, hi, clo):
        return (clo != kt) & (lo + 1 < hi)

    def search_cond(st):
        it, lo, hi, clo, chi, flo, fhi, side = st
        return (it < SEARCH_MAX_ITERS) & (jnp.sum(active(lo, hi, clo).astype(f32)) > 0.0)

    def search_step(st):
        it, lo, hi, clo, chi, flo, fhi, side = st
        act = active(lo, hi, clo)
        lov = _key_to_f32(lo)
        hiv = _key_to_f32(hi - 1)
        guess = _f32_to_key(lov + (hiv - lov) * (flo / (flo - fhi)))
        mid = (lo >> 1) + (hi >> 1) + (lo & hi & 1)
        cand = jnp.where(it < SEARCH_INTERP_ITERS, guess, mid)
        cand = jnp.minimum(jnp.maximum(cand, lo + 1), hi - 1)
        cnt = count_ge(cand)
        up = act & (cnt >= kt)
        dn = act & (cnt < kt)
        f_new = (cnt - kt).astype(f32)
        flo2 = jnp.where(up, f_new + 0.5, jnp.where(dn & (side < 0), flo * 0.5, flo))
        fhi2 = jnp.where(dn, f_new - 0.5, jnp.where(up & (side > 0), fhi * 0.5, fhi))
        side2 = jnp.where(up, 1, jnp.where(dn, -1, side))
        return (it + 1, jnp.where(up, cand, lo), jnp.where(dn, cand, hi), jnp.where(up, cnt, clo),
                jnp.where(dn, cnt, chi), flo2, fhi2, side2)

    lo0 = _f32_to_key(jnp.min(mn8, axis=0, keepdims=True))
    hi0 = _f32_to_key(jnp.max(mx8, axis=0, keepdims=True)) + 1
    clo0 = q0 + lane1 + 1
    chi0 = jnp.zeros((1, QB), i32)
    st = (jnp.int32(0), lo0, hi0, clo0, chi0, (clo0 - kt).astype(f32) + 0.5, (chi0 - kt).astype(f32) - 0.5,
          jnp.zeros((1, QB), i32))
    thr = lax.while_loop(search_cond, lambda s: search_step(search_step(s)), st)[1]
    thr_f = jnp.where(clo0 <= TOPK_MAX, -jnp.inf, _key_to_f32(thr))

    l_ref[...] = jnp.zeros(l_ref.shape, f32)
    acc_ref[...] = jnp.zeros(acc_ref.shape, f32)

    def block_bias(j):
        k0 = pl.multiple_of(j * KB, KB)
        sel = (sc_ref[pl.ds(k0, KB), :] >= thr_f) & (k0 + row <= qidx)
        bias1 = jnp.where(sel, 0.0, NEG).astype(f32)
        return k0, jnp.concatenate([bias1] * N_HEADS, axis=1)

    qf = qT_ref[...].astype(f32)
    qn2 = jnp.max(jnp.sum(qf * qf, axis=0, keepdims=True), axis=1, keepdims=True)
    bounded = jnp.max(qn2 * kn_ref[0:1, 0:1]) <= LOGIT_LIMIT * LOGIT_LIMIT

    def att_bounded(j, carry):
        k0, bias = block_bias(j)
        kblk = kr_ref[pl.ds(k0, KB), :]
        vt = vT_ref[j]
        p = jnp.exp(jnp.dot(kblk, qT_ref[...], preferred_element_type=f32) + bias)
        l_ref[...] += jnp.sum(p, axis=0, keepdims=True)
        acc_ref[...] += jnp.dot(vt, p.astype(bf16), preferred_element_type=f32)
        return carry

    def att_online(j, carry):
        k0, bias = block_bias(j)
        kblk = kr_ref[pl.ds(k0, KB), :]
        vt = vT_ref[j]
        s = jnp.dot(kblk, qT_ref[...], preferred_element_type=f32) + bias
        m_old = m_ref[...]
        m_new = jnp.maximum(m_old, jnp.max(s, axis=0, keepdims=True))
        p = jnp.exp(s - m_new)
        alpha = jnp.exp(m_old - m_new)
        l_ref[...] = l_ref[...] * alpha + jnp.sum(p, axis=0, keepdims=True)
        m_ref[...] = m_new
        acc_ref[...] = acc_ref[...] * alpha + jnp.dot(vt, p.astype(bf16), preferred_element_type=f32)
        return carry

    @pl.when(bounded)
    def _():
        lax.fori_loop(0, nkb, att_bounded, 0)

    @pl.when(jnp.logical_not(bounded))
    def _():
        m_ref[...] = jnp.full(m_ref.shape, NEG, f32)
        lax.fori_loop(0, nkb, att_online, 0)

    out_t = acc_ref[...] / l_ref[...]
    for h in range(N_HEADS):
        o_ref[:, h * HEAD_DIM:(h + 1) * HEAD_DIM] = out_t[:, h * QB:(h + 1) * QB].T.astype(o_ref.dtype)


def _attention(qT, qiT, wT, kr, kir, vT, batch, seq):
    nq = seq // QB
    n = batch * seq
    return pl.pallas_call(
        _attn_kernel,
        out_shape=jax.ShapeDtypeStruct((n, N_HEADS * HEAD_DIM), bf16),
        grid=(batch, nq),
        in_specs=[pl.BlockSpec((None, HEAD_DIM, N_HEADS * QB), lambda b, i: (b * nq + i, 0, 0)),
                  pl.BlockSpec((None, IDX_DIM, IDX_HEADS * QB), lambda b, i: (b * nq + i, 0, 0)),
                  pl.BlockSpec((None, IDX_HEADS, QB), lambda b, i: (b * nq + i, 0, 0)),
                  pl.BlockSpec((seq, HEAD_DIM), lambda b, i: (b, 0)),
                  pl.BlockSpec((seq, LANES), lambda b, i: (b, 0)),
                  pl.BlockSpec((seq // KB, HEAD_DIM, KB), lambda b, i: (b, 0, 0))],
        out_specs=pl.BlockSpec((QB, N_HEADS * HEAD_DIM), lambda b, i: (b * nq + i, 0)),
        scratch_shapes=[pltpu.VMEM((seq, QB), f32),
                        pltpu.VMEM((HEAD_DIM, N_HEADS * QB), f32),
                        pltpu.VMEM((1, N_HEADS * QB), f32),
                        pltpu.VMEM((1, N_HEADS * QB), f32),
                        pltpu.VMEM((8, LANES), f32)],
        compiler_params=_cparams(("parallel", "arbitrary")),
        name="attn",
    )(qT, qiT, wT, kr, kir, vT)


CT = 256
HALO = 32


def _conv_kernel(u_ref, w_ref, b_ref, g_ref, beta_ref, c_ref, ybuf_ref, cbuf_ref):
    i = pl.program_id(1)

    @pl.when(i == 0)
    def _():
        ybuf_ref[0:HALO, :] = jnp.zeros((HALO, CONV_WIDTH), f32)

    @pl.when(i > 0)
    def _():
        ybuf_ref[0:HALO, :] = ybuf_ref[CT:CT + HALO, :]

    a = u_ref[:, 0:CONV_WIDTH].astype(f32)
    gate = u_ref[:, CONV_WIDTH:2 * CONV_WIDTH].astype(f32)
    ybuf_ref[HALO:HALO + CT, :] = a * jax.nn.sigmoid(gate)

    off = HALO - (CONV_KERNEL - 1)
    for c in range(CONV_WIDTH // LANES):
        cs = slice(c * LANES, (c + 1) * LANES)
        acc = jnp.broadcast_to(b_ref[:, cs], (CT, LANES))
        for ph in range(SUBLANES):
            rows = CT if ph == 0 else CT + SUBLANES
            part = None
            for o in range(ph, off + CONV_KERNEL, SUBLANES):
                if o < off:
                    continue
                term = w_ref[o - off:o - off + 1, cs] * ybuf_ref[o - ph:o - ph + rows, cs]
                part = term if part is None else part + term
            acc = acc + part[ph:ph + CT]
        cbuf_ref[:, cs] = acc

    y = cbuf_ref[...]
    mu = jnp.mean(y, axis=-1, keepdims=True)
    yc = y - mu
    var = jnp.mean(yc * yc, axis=-1, keepdims=True)
    yn = yc * lax.rsqrt(var + EPS) * g_ref[...] + beta_ref[...]
    c_ref[...] = (yn * jax.nn.sigmoid(yn)).astype(c_ref.dtype)


def _conv(z, dw_w, dw_b, ln_g, ln_b, batch, seq):
    nt = seq // CT
    n = batch * seq
    return pl.pallas_call(
        _conv_kernel,
        out_shape=jax.ShapeDtypeStruct((n, CONV_WIDTH), bf16),
        grid=(batch, nt),
        in_specs=[pl.BlockSpec((CT, 2 * CONV_WIDTH), lambda b, i: (b * nt + i, Z_U // 2048)),
                  pl.BlockSpec((32, CONV_WIDTH), lambda b, i: (0, 0)),
                  pl.BlockSpec((1, CONV_WIDTH), lambda b, i: (0, 0)),
                  pl.BlockSpec((1, CONV_WIDTH), lambda b, i: (0, 0)),
                  pl.BlockSpec((1, CONV_WIDTH), lambda b, i: (0, 0))],
        out_specs=pl.BlockSpec((CT, CONV_WIDTH), lambda b, i: (b * nt + i, 0)),
        scratch_shapes=[pltpu.VMEM((CT + HALO, CONV_WIDTH), f32), pltpu.VMEM((CT, CONV_WIDTH), f32)],
        compiler_params=_cparams(("parallel", "arbitrary")),
        name="conv",
    )(z, dw_w, dw_b, ln_g, ln_b)


MT = 256


def _mix_kernel(c_ref, a_ref, gc_ref, ga_ref, x_ref, wco_ref, wao_ref, wout_ref, g2_ref, wr_ref, br_ref,
                h1_ref, route_ref):
    yc = jnp.dot(c_ref[...], wco_ref[...], preferred_element_type=f32)
    ya = jnp.dot(a_ref[...], wao_ref[...], preferred_element_type=f32)
    mix = jax.nn.sigmoid(gc_ref[...].astype(f32)) * yc + jax.nn.sigmoid(ga_ref[...].astype(f32)) * ya
    h1 = x_ref[...] + jnp.dot(mix.astype(bf16), wout_ref[...], preferred_element_type=f32)
    h1_ref[...] = h1

    ms = jnp.mean(h1 * h1, axis=-1, keepdims=True)
    hn = h1 * lax.rsqrt(ms + EPS) * g2_ref[...]
    hn_hi = hn.astype(bf16)
    hn_lo = (hn - hn_hi.astype(f32)).astype(bf16)
    r_hi = jnp.dot(hn_hi, wr_ref[...], preferred_element_type=f32)
    r_lo = jnp.dot(hn_lo, wr_ref[:, 0:LANES], preferred_element_type=f32)
    lg = r_hi[:, 0:LANES] + (r_hi[:, LANES:2 * LANES] + r_lo) + br_ref[...]
    lane = lax.broadcasted_iota(i32, lg.shape, 1)
    big = jnp.int32(1 << 20)

    gmask = lane < N_GROUPS
    gl = jnp.where(gmask, lg, -jnp.inf)
    gmax = jnp.max(gl, axis=-1, keepdims=True)
    gidx = jnp.min(jnp.where(gl == gmax, lane, big), axis=-1, keepdims=True)
    pg = 1.0 / jnp.sum(jnp.where(gmask, jnp.exp(gl - gmax), 0.0), axis=-1, keepdims=True)

    lo = N_GROUPS + EXPERTS_PER_GROUP * gidx
    emask = (lane >= lo) & (lane < lo + EXPERTS_PER_GROUP)
    el = jnp.where(emask, lg, -jnp.inf)
    m0 = jnp.max(el, axis=-1, keepdims=True)
    i0 = jnp.min(jnp.where(el == m0, lane, big), axis=-1, keepdims=True)
    el1 = jnp.where(lane == i0, -jnp.inf, el)
    m1 = jnp.max(el1, axis=-1, keepdims=True)
    i1 = jnp.min(jnp.where(el1 == m1, lane, big), axis=-1, keepdims=True)
    e1x = jnp.exp(m1 - m0)
    w0 = 1.0 / (1.0 + e1x)
    w1 = e1x * w0

    route = jnp.where(lane == 0, (i0 - N_GROUPS).astype(f32), 0.0)
    route = jnp.where(lane == 1, (i1 - N_GROUPS).astype(f32), route)
    route = jnp.where(lane == 2, pg * w0, route)
    route = jnp.where(lane == 3, pg * w1, route)
    route_ref[...] = route


def _mix(c, a, z, x2, wco, wao, wout, g2, wr, br):
    n = x2.shape[0]
    once = functools.partial(pl.BlockSpec, pipeline_mode=pl.Buffered(1))
    return pl.pallas_call(
        _mix_kernel,
        out_shape=(jax.ShapeDtypeStruct((n, D_MODEL), f32), jax.ShapeDtypeStruct((n, LANES), f32)),
        grid=(n // MT,),
        in_specs=[pl.BlockSpec((MT, CONV_WIDTH), lambda i: (i, 0)),
                  pl.BlockSpec((MT, D_MODEL), lambda i: (i, 0)),
                  pl.BlockSpec((MT, D_MODEL), lambda i: (i, Z_GC // 2048)),
                  pl.BlockSpec((MT, D_MODEL), lambda i: (i, Z_GA // 2048)),
                  pl.BlockSpec((MT, D_MODEL), lambda i: (i, 0)),
                  once((CONV_WIDTH, D_MODEL), lambda i: (0, 0)),
                  once((D_MODEL, D_MODEL), lambda i: (0, 0)),
                  once((D_MODEL, D_MODEL), lambda i: (0, 0)),
                  pl.BlockSpec((1, D_MODEL), lambda i: (0, 0)),
                  once((D_MODEL, 2 * LANES), lambda i: (0, 0)),
                  pl.BlockSpec((1, LANES), lambda i: (0, 0))],
        out_specs=(pl.BlockSpec((MT, D_MODEL), lambda i: (i, 0)), pl.BlockSpec((MT, LANES), lambda i: (i, 0))),
        compiler_params=_cparams(("parallel",)),
        name="mix",
    )(c, a, z, z, x2, wco, wao, wout, g2, wr, br)


TR = 256
CM = 256


def _expert_kernel(te_ref, nu_ref, first_ref, wsl_ref, nxe_ref, cur_ref, nxt_ref, h_ref, g2_ref,
                   wg_hbm, wu_hbm, wd_hbm, o_ref,
                   xbuf_ref, wgf_ref, wuf_ref, wdf_ref, wgb_ref, wub_ref, wdb_ref, sem, wsem):
    t = pl.program_id(0)
    nu = nu_ref[0]
    slot = lax.rem(t, 2)

    def issue(idx_ref, dst_slot):
        def body(r, carry):
            pltpu.make_async_copy(h_ref.at[pl.ds(idx_ref[0, r], 1)], xbuf_ref.at[dst_slot, pl.ds(r, 1)],
                                  sem.at[dst_slot]).start()
            return carry

        lax.fori_loop(0, TR, body, 0, unroll=8)

    def weight_copies(e, ws):
        return (pltpu.make_async_copy(wg_hbm.at[e], wgf_ref.at[ws], wsem.at[ws]),
                pltpu.make_async_copy(wu_hbm.at[e], wuf_ref.at[ws], wsem.at[ws]),
                pltpu.make_async_copy(wd_hbm.at[e], wdf_ref.at[ws], wsem.at[ws]))

    @pl.when(t == 0)
    def _():
        for cp in weight_copies(te_ref[0], 0):
            cp.start(priority=1)
        issue(cur_ref, 0)

    @pl.when(t + 1 < nu)
    def _():
        issue(nxt_ref, 1 - slot)

    @pl.when(t < nu)
    def _():
        @pl.when(first_ref[t] == 1)
        def _():
            ws = wsl_ref[t]
            for cp in weight_copies(te_ref[t], ws):
                cp.wait()

            @pl.when(nxe_ref[t] >= 0)
            def _():
                for cp in weight_copies(nxe_ref[t], 1 - ws):
                    cp.start(priority=1)

            wgb_ref[...] = wgf_ref[ws].astype(bf16)
            wub_ref[...] = wuf_ref[ws].astype(bf16)
            wdb_ref[...] = wdf_ref[ws].astype(bf16)

        pltpu.make_async_copy(h_ref.at[pl.ds(0, TR)], xbuf_ref.at[slot], sem.at[slot]).wait()
        x = xbuf_ref[slot]
        ms = jnp.mean(x * x, axis=-1, keepdims=True)
        hn = (x * lax.rsqrt(ms + EPS) * g2_ref[...]).astype(bf16)
        gate = jnp.dot(hn, wgb_ref[...], preferred_element_type=f32)
        up = jnp.dot(hn, wub_ref[...], preferred_element_type=f32)
        act = (gate * jax.nn.sigmoid(gate) * up).astype(bf16)
        o_ref[...] = jnp.dot(act, wdb_ref[...], preferred_element_type=f32)

    @pl.when(t >= nu)
    def _():
        o_ref[...] = jnp.zeros(o_ref.shape, o_ref.dtype)


def _experts(tile_expert, n_used, first, wslot, next_expert, slot_token, h1, g2, wg, wu, wd):
    s = slot_token.shape[0]
    ntiles = s // TR
    idx3 = slot_token.reshape(ntiles, 1, TR)
    return pl.pallas_call(
        _expert_kernel,
        out_shape=jax.ShapeDtypeStruct((s, D_MODEL), f32),
        grid_spec=pltpu.PrefetchScalarGridSpec(
            num_scalar_prefetch=5,
            grid=(ntiles,),
            in_specs=[pl.BlockSpec((None, 1, TR), lambda t, *_: (t, 0, 0), memory_space=pltpu.SMEM),
                      pl.BlockSpec((None, 1, TR), lambda t, *_: (jnp.minimum(t + 1, ntiles - 1), 0, 0),
                                   memory_space=pltpu.SMEM),
                      pl.BlockSpec(memory_space=pl.ANY),
                      pl.BlockSpec((1, D_MODEL), lambda t, *_: (0, 0)),
                      pl.BlockSpec(memory_space=pl.ANY),
                      pl.BlockSpec(memory_space=pl.ANY),
                      pl.BlockSpec(memory_space=pl.ANY)],
            out_specs=pl.BlockSpec((TR, D_MODEL), lambda t, *_: (t, 0)),
            scratch_shapes=[pltpu.VMEM((2, TR, D_MODEL), f32),
                            pltpu.VMEM((2, D_MODEL, EXPERT_FF), f32),
                            pltpu.VMEM((2, D_MODEL, EXPERT_FF), f32),
                            pltpu.VMEM((2, EXPERT_FF, D_MODEL), f32),
                            pltpu.VMEM((D_MODEL, EXPERT_FF), bf16),
                            pltpu.VMEM((D_MODEL, EXPERT_FF), bf16),
                            pltpu.VMEM((EXPERT_FF, D_MODEL), bf16),
                            pltpu.SemaphoreType.DMA((2,)),
                            pltpu.SemaphoreType.DMA((2,))]),
        compiler_params=_cparams(("arbitrary",)),
        name="moe_experts",
    )(tile_expert, n_used, first, wslot, next_expert, idx3, idx3, h1, g2, wg, wu, wd)


def _combine_kernel(cur_ref, nxt_ref, h1_ref, route_ref, eo_ref, o_ref, buf_ref, sem):
    i = pl.program_id(0)
    slot = lax.rem(i, 2)

    def issue(idx_ref, dst_slot):
        def body(r, carry):
            pltpu.make_async_copy(eo_ref.at[pl.ds(idx_ref[0, r], 1)], buf_ref.at[dst_slot, 0, pl.ds(r, 1)],
                                  sem.at[dst_slot]).start()
            pltpu.make_async_copy(eo_ref.at[pl.ds(idx_ref[0, CM + r], 1)], buf_ref.at[dst_slot, 1, pl.ds(r, 1)],
                                  sem.at[dst_slot]).start()
            return carry

        lax.fori_loop(0, CM, body, 0, unroll=8)

    @pl.when(i == 0)
    def _():
        issue(cur_ref, 0)

    @pl.when(i + 1 < pl.num_programs(0))
    def _():
        issue(nxt_ref, 1 - slot)

    pltpu.make_async_copy(eo_ref.at[pl.ds(0, CM)], buf_ref.at[slot, 0], sem.at[slot]).wait()
    pltpu.make_async_copy(eo_ref.at[pl.ds(0, CM)], buf_ref.at[slot, 1], sem.at[slot]).wait()
    c0 = route_ref[:, 2:3]
    c1 = route_ref[:, 3:4]
    o_ref[...] = h1_ref[...] + c0 * buf_ref[slot, 0] + c1 * buf_ref[slot, 1]


def _combine(pos01, h1, route, eo):
    n = h1.shape[0]
    nt = n // CM
    return pl.pallas_call(
        _combine_kernel,
        out_shape=jax.ShapeDtypeStruct((n, D_MODEL), f32),
        grid=(nt,),
        in_specs=[pl.BlockSpec((None, 1, 2 * CM), lambda i: (i, 0, 0), memory_space=pltpu.SMEM),
                  pl.BlockSpec((None, 1, 2 * CM), lambda i: (jnp.minimum(i + 1, nt - 1), 0, 0),
                               memory_space=pltpu.SMEM),
                  pl.BlockSpec((CM, D_MODEL), lambda i: (i, 0)),
                  pl.BlockSpec((CM, LANES), lambda i: (i, 0)),
                  pl.BlockSpec(memory_space=pl.ANY)],
        out_specs=pl.BlockSpec((CM, D_MODEL), lambda i: (i, 0)),
        scratch_shapes=[pltpu.VMEM((2, 2, CM, D_MODEL), f32), pltpu.SemaphoreType.DMA((2,))],
        compiler_params=_cparams(("arbitrary",)),
        name="moe_combine",
    )(pos01, pos01, h1, route, eo)


def _route_tables(route, n):
    e = route[:, 0:2].astype(i32).reshape(-1)
    onehot = (e[:, None] == jnp.arange(N_EXPERTS, dtype=i32)[None, :]).astype(i32)
    csum = jnp.cumsum(onehot, axis=0)
    rank = jnp.take_along_axis(csum, e[:, None], axis=1)[:, 0] - 1
    counts = csum[-1]
    ntile = (counts + TR - 1) // TR
    tile_end = jnp.cumsum(ntile)
    tile_start = tile_end - ntile
    pos = tile_start[e] * TR + rank
    s = 2 * n + N_EXPERTS * TR
    slot_token = jnp.zeros((s,), i32).at[pos].set(jnp.arange(2 * n, dtype=i32) // 2)
    n_used = tile_end[-1:]
    t = jnp.minimum(jnp.arange(s // TR, dtype=i32), n_used[0] - 1)
    tile_expert = jnp.sum((tile_end[None, :] <= t[:, None]).astype(i32), axis=1)
    used = jnp.arange(s // TR, dtype=i32) < n_used[0]
    prev = jnp.concatenate([jnp.full((1,), -1, i32), tile_expert[:-1]])
    first = ((tile_expert != prev) & used).astype(i32)
    wslot = (jnp.cumsum(first) - 1) % 2
    nxt_tile = tile_end[tile_expert]
    next_expert = jnp.where((first == 1) & (nxt_tile < n_used[0]),
                            tile_expert[jnp.minimum(nxt_tile, s // TR - 1)], -1)
    pos01 = jnp.concatenate([pos[0::2].reshape(n // CM, 1, CM), pos[1::2].reshape(n // CM, 1, CM)], axis=2)
    return slot_token, tile_expert, n_used.astype(i32), first, wslot.astype(i32), next_expert.astype(i32), pos01


def _pack_w_in(w):
    u, q, k, v, qi, ki, wi, gc, ga = jnp.split(w, [2048, 4096, 4224, 4352, 5376, 5440, 5456, 7504], axis=1)
    pad = jnp.zeros((w.shape[0], LANES - IDX_DIM - IDX_HEADS), w.dtype)
    return jnp.concatenate([u, q, gc, ga, qi, k, v, ki, wi, pad], axis=1).astype(bf16)


def _lane_inv(rot_dim, period):
    inv = ROPE_THETA ** (-jnp.arange(0, rot_dim, 2, dtype=f32) / rot_dim)
    lane = jnp.arange(LANES)
    lp = lane % period
    return jnp.where(lp < rot_dim, inv[lp % (rot_dim // 2)], 0.0).astype(f32)[None, :]


def kernel(x, positions, attn_norm_g, w_in, conv_dw_w, conv_dw_b, conv_ln_g, conv_ln_b, w_conv_out, q_norm_g, k_norm_g, w_attn_o, w_out, ffn_norm_g, w_router_group, b_router_group, w_router_expert, b_router_expert, w_exp_gate, w_exp_up, w_exp_down):
    batch, seq, d = x.shape
    n = batch * seq
    h = x.reshape(n, d)
    pos2 = positions.reshape(n, 1)
    inv_a = _lane_inv(HEAD_DIM // ROT_FRACTION, LANES)
    inv_i = _lane_inv(IDX_DIM // ROT_FRACTION, IDX_DIM)
    for l in range(attn_norm_g.shape[0]):
        z = _inproj(h, attn_norm_g[l][None, :], _pack_w_in(w_in[l]))
        qT, qiT, wT, kr, kir, vT = _prep(z, pos2, inv_a, inv_i, q_norm_g[l][None, :], k_norm_g[l][None, :])
        a = _attention(qT, qiT, wT, kr, kir, vT, batch, seq)
        dw = jnp.concatenate([conv_dw_w[l], jnp.zeros((1, CONV_WIDTH), f32)], axis=0)
        c = _conv(z, dw, conv_dw_b[l][None, :], conv_ln_g[l][None, :], conv_ln_b[l][None, :], batch, seq)
        wr = jnp.concatenate([w_router_group[l], w_router_expert[l],
                              jnp.zeros((d, LANES - N_GROUPS - N_EXPERTS), f32)], axis=1)
        br = jnp.concatenate([b_router_group[l], b_router_expert[l],
                              jnp.zeros((LANES - N_GROUPS - N_EXPERTS,), f32)])[None, :]
        wr_hi = wr.astype(bf16)
        wr2 = jnp.concatenate([wr_hi, (wr - wr_hi.astype(f32)).astype(bf16)], axis=1)
        g2 = ffn_norm_g[l][None, :]
        h1, route = _mix(c, a, z, h, w_conv_out[l].astype(bf16), w_attn_o[l].astype(bf16), w_out[l].astype(bf16),
                         g2, wr2, br)
        slot_token, tile_expert, n_used, first, wslot, next_expert, pos01 = _route_tables(route, n)
        eo = _experts(tile_expert, n_used, first, wslot, next_expert, slot_token, h1, g2,
                      w_exp_gate[l], w_exp_up[l], w_exp_down[l])
        h = _combine(pos01, h1, route, eo)
    return h.reshape(batch, seq, d)
```

```python
import functools

import jax
import jax.numpy as jnp
from jax import lax
from jax.experimental import pallas as pl
from jax.experimental.pallas import tpu as pltpu

f32 = jnp.float32
bf16 = jnp.bfloat16
i32 = jnp.int32

D_MODEL = 2048
CONV_WIDTH = 1024
CONV_KERNEL = 31
N_HEADS = 16
HEAD_DIM = 128
ROPE_THETA = 500000.0
ROT_FRACTION = 4
IDX_HEADS = 16
IDX_DIM = 64
TOPK_MAX = 256
IDX_SCALE = IDX_DIM ** -0.5 * IDX_HEADS ** -0.5
ATTN_SCALE = HEAD_DIM ** -0.5
N_GROUPS = 4
EXPERTS_PER_GROUP = 8
N_EXPERTS = 32
EXPERT_FF = 512
EPS = 1e-6

LANES = 128
SUBLANES = 8
NEG = -1e30
LOG2E = 1.4426950408889634
LOGIT_LIMIT = 86.0
SEARCH_INTERP_ITERS = 40
SEARCH_MAX_ITERS = SEARCH_INTERP_ITERS + 34

ZW = 9600
Z_U, Z_Q, Z_GC, Z_GA, Z_QI, Z_K, Z_V, Z_KIW = 0, 2048, 4096, 6144, 8192, 9216, 9344, 9472

QB = 256
KB = 512
VMEM_LIMIT = 56 * 1024 * 1024


def _cparams(sem, flags=None):
    return pltpu.CompilerParams(dimension_semantics=sem, vmem_limit_bytes=VMEM_LIMIT, flags=flags)


def _inproj_kernel(x_ref, g_ref, w_ref, z_ref, hn_ref):
    @pl.when(pl.program_id(1) == 0)
    def _():
        x = x_ref[...]
        ms = jnp.mean(x * x, axis=-1, keepdims=True)
        hn_ref[...] = (x * lax.rsqrt(ms + EPS) * g_ref[...]).astype(bf16)

    z_ref[...] = jnp.dot(hn_ref[...], w_ref[...], preferred_element_type=f32).astype(z_ref.dtype)


def _inproj(x2, g, w_packed, tm=512, tn=1920):
    n = x2.shape[0]
    return pl.pallas_call(
        _inproj_kernel,
        out_shape=jax.ShapeDtypeStruct((n, ZW), bf16),
        grid=(n // tm, ZW // tn),
        in_specs=[pl.BlockSpec((tm, D_MODEL), lambda i, j: (i, 0)),
                  pl.BlockSpec((1, D_MODEL), lambda i, j: (0, 0)),
                  pl.BlockSpec((D_MODEL, tn), lambda i, j: (0, j))],
        out_specs=pl.BlockSpec((tm, tn), lambda i, j: (i, j)),
        scratch_shapes=[pltpu.VMEM((tm, D_MODEL), bf16)],
        compiler_params=_cparams(("parallel", "arbitrary")),
        name="inproj",
    )(x2, g, w_packed)


PT = 512


def _prep_kernel(q_ref, qi_ref, k_ref, v_ref, kiw_ref, pos_ref, inva_ref, invi_ref, qg_ref, kg_ref,
                 qT_ref, qiT_ref, wT_ref, kr_ref, kir_ref, vT_ref):
    lane = lax.broadcasted_iota(i32, (PT, LANES), 1)
    pos = pos_ref[...].astype(f32)
    ang_a = pos * inva_ref[...]
    ca, sa = jnp.cos(ang_a), jnp.sin(ang_a)
    s1a = jnp.where(lane < 16, -sa, 0.0)
    s2a = jnp.where(lane >= 16, sa, 0.0)
    ang_i = pos * invi_ref[...]
    ci, si = jnp.cos(ang_i), jnp.sin(ang_i)
    l64 = lane & 63
    s1i = jnp.where(l64 < 8, -si, 0.0)
    s2i = jnp.where(l64 >= 8, si, 0.0)

    def rope_a(x):
        return x * ca + pltpu.roll(x, 112, 1) * s1a + pltpu.roll(x, 16, 1) * s2a

    def rope_i(x):
        return x * ci + pltpu.roll(x, 120, 1) * s1i + pltpu.roll(x, 8, 1) * s2i

    def rms(x, g):
        ms = jnp.mean(x * x, axis=-1, keepdims=True)
        return x * lax.rsqrt(ms + EPS) * g

    for h in range(N_HEADS):
        qh = q_ref[:, h * HEAD_DIM:(h + 1) * HEAD_DIM].astype(f32)
        qh = rope_a(rms(qh, qg_ref[...])) * (ATTN_SCALE * LOG2E)
        for sb in range(PT // QB):
            qT_ref[sb, :, h * QB:(h + 1) * QB] = qh[sb * QB:(sb + 1) * QB, :].T.astype(bf16)

    for hp in range(IDX_HEADS // 2):
        x = rope_i(qi_ref[:, hp * LANES:(hp + 1) * LANES].astype(f32))
        for sb in range(PT // QB):
            xt = x[sb * QB:(sb + 1) * QB, :].T
            qiT_ref[sb, :, (2 * hp) * QB:(2 * hp + 1) * QB] = xt[0:IDX_DIM].astype(bf16)
            qiT_ref[sb, :, (2 * hp + 1) * QB:(2 * hp + 2) * QB] = xt[IDX_DIM:2 * IDX_DIM].astype(bf16)

    kiw = kiw_ref[...].astype(f32)
    w = pltpu.roll(kiw, 64, 1) * IDX_SCALE
    for sb in range(PT // QB):
        wT_ref[sb] = w[sb * QB:(sb + 1) * QB, :].T[0:IDX_HEADS]

    kir_ref[...] = jnp.where(lane < IDX_DIM, rope_i(kiw), 0.0).astype(bf16)
    kr_ref[...] = rope_a(rms(k_ref[...].astype(f32), kg_ref[...])).astype(bf16)
    vT_ref[0] = v_ref[...].astype(f32).T.astype(bf16)


def _prep(z, pos2, inv_a, inv_i, qg, kg):
    n = z.shape[0]
    nblk = n // QB
    cb = lambda c, w: c // w
    return pl.pallas_call(
        _prep_kernel,
        out_shape=(jax.ShapeDtypeStruct((nblk, HEAD_DIM, N_HEADS * QB), bf16),
                   jax.ShapeDtypeStruct((nblk, IDX_DIM, IDX_HEADS * QB), bf16),
                   jax.ShapeDtypeStruct((nblk, IDX_HEADS, QB), f32),
                   jax.ShapeDtypeStruct((n, HEAD_DIM), bf16),
                   jax.ShapeDtypeStruct((n, LANES), bf16),
                   jax.ShapeDtypeStruct((n // KB, HEAD_DIM, KB), bf16)),
        grid=(n // PT,),
        in_specs=[pl.BlockSpec((PT, 2048), lambda i: (i, cb(Z_Q, 2048))),
                  pl.BlockSpec((PT, 1024), lambda i: (i, cb(Z_QI, 1024))),
                  pl.BlockSpec((PT, LANES), lambda i: (i, cb(Z_K, LANES))),
                  pl.BlockSpec((PT, LANES), lambda i: (i, cb(Z_V, LANES))),
                  pl.BlockSpec((PT, LANES), lambda i: (i, cb(Z_KIW, LANES))),
                  pl.BlockSpec((PT, 1), lambda i: (i, 0)),
                  pl.BlockSpec((1, LANES), lambda i: (0, 0)),
                  pl.BlockSpec((1, LANES), lambda i: (0, 0)),
                  pl.BlockSpec((1, LANES), lambda i: (0, 0)),
                  pl.BlockSpec((1, LANES), lambda i: (0, 0))],
        out_specs=(pl.BlockSpec((PT // QB, HEAD_DIM, N_HEADS * QB), lambda i: (i, 0, 0)),
                   pl.BlockSpec((PT // QB, IDX_DIM, IDX_HEADS * QB), lambda i: (i, 0, 0)),
                   pl.BlockSpec((PT // QB, IDX_HEADS, QB), lambda i: (i, 0, 0)),
                   pl.BlockSpec((PT, HEAD_DIM), lambda i: (i, 0)),
                   pl.BlockSpec((PT, LANES), lambda i: (i, 0)),
                   pl.BlockSpec((1, HEAD_DIM, KB), lambda i: (i, 0, 0))),
        compiler_params=_cparams(("parallel",)),
        name="prep",
    )(z, z, z, z, z, pos2, inv_a, inv_i, qg, kg)


def _f32_to_key(x):
    bits = pltpu.bitcast(x, i32)
    return bits ^ ((bits >> 31) & jnp.int32(0x7FFFFFFF))


def _key_to_f32(k):
    return pltpu.bitcast(k ^ ((k >> 31) & jnp.int32(0x7FFFFFFF)), f32)


def _attn_kernel(qT_ref, qiT_ref, wT_ref, kr_ref, kir_ref, vT_ref, o_ref, sc_ref, acc_ref, m_ref, l_ref, kn_ref):
    qb = pl.program_id(1)
    q0 = qb * QB

    @pl.when(qb == 0)
    def _():
        def kn_body(j, mx):
            kf = kr_ref[pl.ds(pl.multiple_of(j * KB, KB), KB), :].astype(f32)
            return jnp.maximum(mx, jnp.max(jnp.sum(kf * kf, axis=1, keepdims=True), axis=0, keepdims=True))

        kn = lax.fori_loop(0, kr_ref.shape[0] // KB, kn_body, jnp.zeros((1, 1), f32))
        kn_ref[...] = jnp.broadcast_to(kn, kn_ref.shape)

    nkb = (q0 + QB - 1) // KB + 1
    qidx = q0 + lax.broadcasted_iota(i32, (KB, QB), 1)
    row = lax.broadcasted_iota(i32, (KB, QB), 0)

    def score_body(j, carry):
        k0 = pl.multiple_of(j * KB, KB)
        kblk = kir_ref[pl.ds(k0, KB), 0:IDX_DIM]
        sc = jnp.zeros((KB, QB), f32)
        for hp in range(IDX_HEADS // 2):
            s = jnp.dot(kblk, qiT_ref[:, hp * 2 * QB:(hp + 1) * 2 * QB], preferred_element_type=f32)
            s = jnp.maximum(s, 0.0)
            sc = sc + s[:, 0:QB] * wT_ref[2 * hp:2 * hp + 1, :] + s[:, QB:2 * QB] * wT_ref[2 * hp + 1:2 * hp + 2, :]
        causal = k0 + row <= qidx
        sc_ref[pl.ds(k0, KB), :] = jnp.where(causal, sc, -jnp.inf)
        mn, mx = carry
        mn = jnp.minimum(mn, jnp.min(jnp.where(causal, sc, jnp.inf).reshape(KB // 8, 8, QB), axis=0))
        mx = jnp.maximum(mx, jnp.max(jnp.where(causal, sc, -jnp.inf).reshape(KB // 8, 8, QB), axis=0))
        return mn, mx

    mn8, mx8 = lax.fori_loop(0, nkb, score_body,
                             (jnp.full((8, QB), jnp.inf, f32), jnp.full((8, QB), -jnp.inf, f32)))

    lane1 = lax.broadcasted_iota(i32, (1, QB), 1)
    kt = jnp.minimum(TOPK_MAX, q0 + lane1 + 1)

    def count_ge(cand):
        candf = _key_to_f32(cand)

        def cnt_body(j, c):
            k0 = pl.multiple_of(j * KB, KB)
            ge = (sc_ref[pl.ds(k0, KB), :] >= candf).astype(i32)
            return c + jnp.sum(ge.reshape(KB // 8, 8, QB), axis=0)

        c8 = lax.fori_loop(0, nkb, cnt_body, jnp.zeros((8, QB), i32))
        return jnp.sum(c8, axis=0, keepdims=True)

    def active(lo, hi, clo):
        return (clo != kt) & (lo + 1 < hi)

    def search_cond(st):
        it, lo, hi, clo, chi, flo, fhi, side = st
        return (it < SEARCH_MAX_ITERS) & (jnp.sum(active(lo, hi, clo).astype(f32)) > 0.0)

    def search_step(st):
        it, lo, hi, clo, chi, flo, fhi, side = st
        act = active(lo, hi, clo)
        lov = _key_to_f32(lo)
        hiv = _key_to_f32(hi - 1)
        guess = _f32_to_key(lov + (hiv - lov) * (flo / (flo - fhi)))
        mid = (lo >> 1) + (hi >> 1) + (lo & hi & 1)
        cand = jnp.where(it < SEARCH_INTERP_ITERS, guess, mid)
        cand = jnp.minimum(jnp.maximum(cand, lo + 1), hi - 1)
        cnt = count_ge(cand)
        up = act & (cnt >= kt)
        dn = act & (cnt < kt)
        f_new = (cnt - kt).astype(f32)
        flo2 = jnp.where(up, f_new + 0.5, jnp.where(dn & (side < 0), flo * 0.5, flo))
        fhi2 = jnp.where(dn, f_new - 0.5, jnp.where(up & (side > 0), fhi * 0.5, fhi))
        side2 = jnp.where(up, 1, jnp.where(dn, -1, side))
        return (it + 1, jnp.where(up, cand, lo), jnp.where(dn, cand, hi), jnp.where(up, cnt, clo),
                jnp.where(dn, cnt, chi), flo2, fhi2, side2)

    lo0 = _f32_to_key(jnp.min(mn8, axis=0, keepdims=True))
    hi0 = _f32_to_key(jnp.max(mx8, axis=0, keepdims=True)) + 1
    clo0 = q0 + lane1 + 1
    chi0 = jnp.zeros((1, QB), i32)
    st = (jnp.int32(0), lo0, hi0, clo0, chi0, (clo0 - kt).astype(f32) + 0.5, (chi0 - kt).astype(f32) - 0.5,
          jnp.zeros((1, QB), i32))
    thr = lax.while_loop(search_cond, lambda s: search_step(search_step(s)), st)[1]
    thr_f = jnp.where(clo0 <= TOPK_MAX, -jnp.inf, _key_to_f32(thr))

    l_ref[...] = jnp.zeros(l_ref.shape, f32)
    acc_ref[...] = jnp.zeros(acc_ref.shape, f32)

    def block_bias(j):
        k0 = pl.multiple_of(j * KB, KB)
        sel = (sc_ref[pl.ds(k0, KB), :] >= thr_f) & (k0 + row <= qidx)
        bias1 = jnp.where(sel, 0.0, NEG).astype(f32)
        return k0, jnp.concatenate([bias1] * N_HEADS, axis=1)

    qf = qT_ref[...].astype(f32)
    qn2 = jnp.max(jnp.sum(qf * qf, axis=0, keepdims=True), axis=1, keepdims=True)
    bounded = jnp.max(qn2 * kn_ref[0:1, 0:1]) <= LOGIT_LIMIT * LOGIT_LIMIT

    def att_bounded(j, carry):
        k0, bias = block_bias(j)
        kblk = kr_ref[pl.ds(k0, KB), :]
        vt = vT_ref[j]
        p = jnp.exp2(jnp.dot(kblk, qT_ref[...], preferred_element_type=f32) + bias)
        l_ref[...] += jnp.sum(p, axis=0, keepdims=True)
        acc_ref[...] += jnp.dot(vt, p.astype(bf16), preferred_element_type=f32)
        return carry

    def att_online(j, carry):
        k0, bias = block_bias(j)
        kblk = kr_ref[pl.ds(k0, KB), :]
        vt = vT_ref[j]
        s = jnp.dot(kblk, qT_ref[...], preferred_element_type=f32) + bias
        m_old = m_ref[...]
        m_new = jnp.maximum(m_old, jnp.max(s, axis=0, keepdims=True))
        p = jnp.exp2(s - m_new)
        alpha = jnp.exp2(m_old - m_new)
        l_ref[...] = l_ref[...] * alpha + jnp.sum(p, axis=0, keepdims=True)
        m_ref[...] = m_new
        acc_ref[...] = acc_ref[...] * alpha + jnp.dot(vt, p.astype(bf16), preferred_element_type=f32)
        return carry

    @pl.when(bounded)
    def _():
        lax.fori_loop(0, nkb, att_bounded, 0)

    @pl.when(jnp.logical_not(bounded))
    def _():
        m_ref[...] = jnp.full(m_ref.shape, NEG, f32)
        lax.fori_loop(0, nkb, att_online, 0)

    out_t = acc_ref[...] / l_ref[...]
    for h in range(N_HEADS):
        o_ref[:, h * HEAD_DIM:(h + 1) * HEAD_DIM] = out_t[:, h * QB:(h + 1) * QB].T.astype(o_ref.dtype)


def _attention(qT, qiT, wT, kr, kir, vT, batch, seq):
    nq = seq // QB
    n = batch * seq
    return pl.pallas_call(
        _attn_kernel,
        out_shape=jax.ShapeDtypeStruct((n, N_HEADS * HEAD_DIM), bf16),
        grid=(batch, nq),
        in_specs=[pl.BlockSpec((None, HEAD_DIM, N_HEADS * QB), lambda b, i: (b * nq + i, 0, 0)),
                  pl.BlockSpec((None, IDX_DIM, IDX_HEADS * QB), lambda b, i: (b * nq + i, 0, 0)),
                  pl.BlockSpec((None, IDX_HEADS, QB), lambda b, i: (b * nq + i, 0, 0)),
                  pl.BlockSpec((seq, HEAD_DIM), lambda b, i: (b, 0)),
                  pl.BlockSpec((seq, LANES), lambda b, i: (b, 0)),
                  pl.BlockSpec((seq // KB, HEAD_DIM, KB), lambda b, i: (b, 0, 0))],
        out_specs=pl.BlockSpec((QB, N_HEADS * HEAD_DIM), lambda b, i: (b * nq + i, 0)),
        scratch_shapes=[pltpu.VMEM((seq, QB), f32),
                        pltpu.VMEM((HEAD_DIM, N_HEADS * QB), f32),
                        pltpu.VMEM((1, N_HEADS * QB), f32),
                        pltpu.VMEM((1, N_HEADS * QB), f32),
                        pltpu.VMEM((8, LANES), f32)],
        compiler_params=_cparams(("parallel", "arbitrary")),
        name="attn",
    )(qT, qiT, wT, kr, kir, vT)


CT = 256
HALO = 32


def _conv_kernel(u_ref, w_ref, b_ref, g_ref, beta_ref, c_ref, ybuf_ref, cbuf_ref):
    i = pl.program_id(1)

    @pl.when(i == 0)
    def _():
        ybuf_ref[0:HALO, :] = jnp.zeros((HALO, CONV_WIDTH), f32)

    @pl.when(i > 0)
    def _():
        ybuf_ref[0:HALO, :] = ybuf_ref[CT:CT + HALO, :]

    a = u_ref[:, 0:CONV_WIDTH].astype(f32)
    gate = u_ref[:, CONV_WIDTH:2 * CONV_WIDTH].astype(f32)
    ybuf_ref[HALO:HALO + CT, :] = a * jax.nn.sigmoid(gate)

    off = HALO - (CONV_KERNEL - 1)
    for c in range(CONV_WIDTH // LANES):
        cs = slice(c * LANES, (c + 1) * LANES)
        acc = jnp.broadcast_to(b_ref[:, cs], (CT, LANES))
        for ph in range(SUBLANES):
            rows = CT if ph == 0 else CT + SUBLANES
            part = None
            for o in range(ph, off + CONV_KERNEL, SUBLANES):
                if o < off:
                    continue
                term = w_ref[o - off:o - off + 1, cs] * ybuf_ref[o - ph:o - ph + rows, cs]
                part = term if part is None else part + term
            acc = acc + part[ph:ph + CT]
        cbuf_ref[:, cs] = acc

    y = cbuf_ref[...]
    mu = jnp.mean(y, axis=-1, keepdims=True)
    yc = y - mu
    var = jnp.mean(yc * yc, axis=-1, keepdims=True)
    yn = yc * lax.rsqrt(var + EPS) * g_ref[...] + beta_ref[...]
    c_ref[...] = (yn * jax.nn.sigmoid(yn)).astype(c_ref.dtype)


def _conv(z, dw_w, dw_b, ln_g, ln_b, batch, seq):
    nt = seq // CT
    n = batch * seq
    return pl.pallas_call(
        _conv_kernel,
        out_shape=jax.ShapeDtypeStruct((n, CONV_WIDTH), bf16),
        grid=(batch, nt),
        in_specs=[pl.BlockSpec((CT, 2 * CONV_WIDTH), lambda b, i: (b * nt + i, Z_U // 2048)),
                  pl.BlockSpec((32, CONV_WIDTH), lambda b, i: (0, 0)),
                  pl.BlockSpec((1, CONV_WIDTH), lambda b, i: (0, 0)),
                  pl.BlockSpec((1, CONV_WIDTH), lambda b, i: (0, 0)),
                  pl.BlockSpec((1, CONV_WIDTH), lambda b, i: (0, 0))],
        out_specs=pl.BlockSpec((CT, CONV_WIDTH), lambda b, i: (b * nt + i, 0)),
        scratch_shapes=[pltpu.VMEM((CT + HALO, CONV_WIDTH), f32), pltpu.VMEM((CT, CONV_WIDTH), f32)],
        compiler_params=_cparams(("parallel", "arbitrary")),
        name="conv",
    )(z, dw_w, dw_b, ln_g, ln_b)


MT = 256


def _mix_kernel(c_ref, a_ref, gc_ref, ga_ref, x_ref, wco_ref, wao_ref, wout_ref, g2_ref, wr_ref, br_ref,
                h1_ref, route_ref):
    yc = jnp.dot(c_ref[...], wco_ref[...], preferred_element_type=f32)
    ya = jnp.dot(a_ref[...], wao_ref[...], preferred_element_type=f32)
    mix = jax.nn.sigmoid(gc_ref[...].astype(f32)) * yc + jax.nn.sigmoid(ga_ref[...].astype(f32)) * ya
    h1 = x_ref[...] + jnp.dot(mix.astype(bf16), wout_ref[...], preferred_element_type=f32)
    h1_ref[...] = h1

    ms = jnp.mean(h1 * h1, axis=-1, keepdims=True)
    hn = h1 * lax.rsqrt(ms + EPS) * g2_ref[...]
    hn_hi = hn.astype(bf16)
    hn_lo = (hn - hn_hi.astype(f32)).astype(bf16)
    r_hi = jnp.dot(hn_hi, wr_ref[...], preferred_element_type=f32)
    r_lo = jnp.dot(hn_lo, wr_ref[:, 0:LANES], preferred_element_type=f32)
    lg = r_hi[:, 0:LANES] + (r_hi[:, LANES:2 * LANES] + r_lo) + br_ref[...]
    lane = lax.broadcasted_iota(i32, lg.shape, 1)
    big = jnp.int32(1 << 20)

    gmask = lane < N_GROUPS
    gl = jnp.where(gmask, lg, -jnp.inf)
    gmax = jnp.max(gl, axis=-1, keepdims=True)
    gidx = jnp.min(jnp.where(gl == gmax, lane, big), axis=-1, keepdims=True)
    pg = 1.0 / jnp.sum(jnp.where(gmask, jnp.exp(gl - gmax), 0.0), axis=-1, keepdims=True)

    lo = N_GROUPS + EXPERTS_PER_GROUP * gidx
    emask = (lane >= lo) & (lane < lo + EXPERTS_PER_GROUP)
    el = jnp.where(emask, lg, -jnp.inf)
    m0 = jnp.max(el, axis=-1, keepdims=True)
    i0 = jnp.min(jnp.where(el == m0, lane, big), axis=-1, keepdims=True)
    el1 = jnp.where(lane == i0, -jnp.inf, el)
    m1 = jnp.max(el1, axis=-1, keepdims=True)
    i1 = jnp.min(jnp.where(el1 == m1, lane, big), axis=-1, keepdims=True)
    e1x = jnp.exp(m1 - m0)
    w0 = 1.0 / (1.0 + e1x)
    w1 = e1x * w0

    route = jnp.where(lane == 0, (i0 - N_GROUPS).astype(f32), 0.0)
    route = jnp.where(lane == 1, (i1 - N_GROUPS).astype(f32), route)
    route = jnp.where(lane == 2, pg * w0, route)
    route = jnp.where(lane == 3, pg * w1, route)
    route_ref[...] = route


def _mix(c, a, z, x2, wco, wao, wout, g2, wr, br):
    n = x2.shape[0]
    once = functools.partial(pl.BlockSpec, pipeline_mode=pl.Buffered(1))
    return pl.pallas_call(
        _mix_kernel,
        out_shape=(jax.ShapeDtypeStruct((n, D_MODEL), f32), jax.ShapeDtypeStruct((n, LANES), f32)),
        grid=(n // MT,),
        in_specs=[pl.BlockSpec((MT, CONV_WIDTH), lambda i: (i, 0)),
                  pl.BlockSpec((MT, D_MODEL), lambda i: (i, 0)),
                  pl.BlockSpec((MT, D_MODEL), lambda i: (i, Z_GC // 2048)),
                  pl.BlockSpec((MT, D_MODEL), lambda i: (i, Z_GA // 2048)),
                  pl.BlockSpec((MT, D_MODEL), lambda i: (i, 0)),
                  once((CONV_WIDTH, D_MODEL), lambda i: (0, 0)),
                  once((D_MODEL, D_MODEL), lambda i: (0, 0)),
                  once((D_MODEL, D_MODEL), lambda i: (0, 0)),
                  pl.BlockSpec((1, D_MODEL), lambda i: (0, 0)),
                  once((D_MODEL, 2 * LANES), lambda i: (0, 0)),
                  pl.BlockSpec((1, LANES), lambda i: (0, 0))],
        out_specs=(pl.BlockSpec((MT, D_MODEL), lambda i: (i, 0)), pl.BlockSpec((MT, LANES), lambda i: (i, 0))),
        compiler_params=_cparams(("parallel",)),
        name="mix",
    )(c, a, z, z, x2, wco, wao, wout, g2, wr, br)


TR = 256
CM = 256


def _expert_kernel(te_ref, nu_ref, first_ref, wsl_ref, nxe_ref, cur_ref, nxt_ref, h_ref, g2_ref,
                   wg_hbm, wu_hbm, wd_hbm, o_ref,
                   xbuf_ref, wgf_ref, wuf_ref, wdf_ref, wgb_ref, wub_ref, wdb_ref, sem, wsem):
    t = pl.program_id(0)
    nu = nu_ref[0]
    slot = lax.rem(t, 2)

    def issue(idx_ref, dst_slot):
        def body(r, carry):
            pltpu.make_async_copy(h_ref.at[pl.ds(idx_ref[0, r], 1)], xbuf_ref.at[dst_slot, pl.ds(r, 1)],
                                  sem.at[dst_slot]).start()
            return carry

        lax.fori_loop(0, TR, body, 0, unroll=8)

    def weight_copies(e, ws):
        return (pltpu.make_async_copy(wg_hbm.at[e], wgf_ref.at[ws], wsem.at[ws]),
                pltpu.make_async_copy(wu_hbm.at[e], wuf_ref.at[ws], wsem.at[ws]),
                pltpu.make_async_copy(wd_hbm.at[e], wdf_ref.at[ws], wsem.at[ws]))

    @pl.when(t == 0)
    def _():
        for cp in weight_copies(te_ref[0], 0):
            cp.start(priority=1)
        issue(cur_ref, 0)

    @pl.when(t + 1 < nu)
    def _():
        issue(nxt_ref, 1 - slot)

    @pl.when(t < nu)
    def _():
        @pl.when(first_ref[t] == 1)
        def _():
            ws = wsl_ref[t]
            for cp in weight_copies(te_ref[t], ws):
                cp.wait()

            @pl.when(nxe_ref[t] >= 0)
            def _():
                for cp in weight_copies(nxe_ref[t], 1 - ws):
                    cp.start(priority=1)

            wgb_ref[...] = wgf_ref[ws].astype(bf16)
            wub_ref[...] = wuf_ref[ws].astype(bf16)
            wdb_ref[...] = wdf_ref[ws].astype(bf16)

        pltpu.make_async_copy(h_ref.at[pl.ds(0, TR)], xbuf_ref.at[slot], sem.at[slot]).wait()
        x = xbuf_ref[slot]
        ms = jnp.mean(x * x, axis=-1, keepdims=True)
        hn = (x * lax.rsqrt(ms + EPS) * g2_ref[...]).astype(bf16)
        gate = jnp.dot(hn, wgb_ref[...], preferred_element_type=f32)
        up = jnp.dot(hn, wub_ref[...], preferred_element_type=f32)
        act = (gate * jax.nn.sigmoid(gate) * up).astype(bf16)
        o_ref[...] = jnp.dot(act, wdb_ref[...], preferred_element_type=f32)

    @pl.when(t >= nu)
    def _():
        o_ref[...] = jnp.zeros(o_ref.shape, o_ref.dtype)


def _experts(tile_expert, n_used, first, wslot, next_expert, slot_token, h1, g2, wg, wu, wd):
    s = slot_token.shape[0]
    ntiles = s // TR
    idx3 = slot_token.reshape(ntiles, 1, TR)
    return pl.pallas_call(
        _expert_kernel,
        out_shape=jax.ShapeDtypeStruct((s, D_MODEL), f32),
        grid_spec=pltpu.PrefetchScalarGridSpec(
            num_scalar_prefetch=5,
            grid=(ntiles,),
            in_specs=[pl.BlockSpec((None, 1, TR), lambda t, *_: (t, 0, 0), memory_space=pltpu.SMEM),
                      pl.BlockSpec((None, 1, TR), lambda t, *_: (jnp.minimum(t + 1, ntiles - 1), 0, 0),
                                   memory_space=pltpu.SMEM),
                      pl.BlockSpec(memory_space=pl.ANY),
                      pl.BlockSpec((1, D_MODEL), lambda t, *_: (0, 0)),
                      pl.BlockSpec(memory_space=pl.ANY),
                      pl.BlockSpec(memory_space=pl.ANY),
                      pl.BlockSpec(memory_space=pl.ANY)],
            out_specs=pl.BlockSpec((TR, D_MODEL), lambda t, *_: (t, 0)),
            scratch_shapes=[pltpu.VMEM((2, TR, D_MODEL), f32),
                            pltpu.VMEM((2, D_MODEL, EXPERT_FF), f32),
                            pltpu.VMEM((2, D_MODEL, EXPERT_FF), f32),
                            pltpu.VMEM((2, EXPERT_FF, D_MODEL), f32),
                            pltpu.VMEM((D_MODEL, EXPERT_FF), bf16),
                            pltpu.VMEM((D_MODEL, EXPERT_FF), bf16),
                            pltpu.VMEM((EXPERT_FF, D_MODEL), bf16),
                            pltpu.SemaphoreType.DMA((2,)),
                            pltpu.SemaphoreType.DMA((2,))]),
        compiler_params=_cparams(("arbitrary",)),
        name="moe_experts",
    )(tile_expert, n_used, first, wslot, next_expert, idx3, idx3, h1, g2, wg, wu, wd)


def _combine_kernel(cur_ref, nxt_ref, h1_ref, route_ref, eo_ref, o_ref, buf_ref, sem):
    i = pl.program_id(0)
    slot = lax.rem(i, 2)

    def issue(idx_ref, dst_slot):
        def body(r, carry):
            pltpu.make_async_copy(eo_ref.at[pl.ds(idx_ref[0, r], 1)], buf_ref.at[dst_slot, 0, pl.ds(r, 1)],
                                  sem.at[dst_slot]).start()
            pltpu.make_async_copy(eo_ref.at[pl.ds(idx_ref[0, CM + r], 1)], buf_ref.at[dst_slot, 1, pl.ds(r, 1)],
                                  sem.at[dst_slot]).start()
            return carry

        lax.fori_loop(0, CM, body, 0, unroll=8)

    @pl.when(i == 0)
    def _():
        issue(cur_ref, 0)

    @pl.when(i + 1 < pl.num_programs(0))
    def _():
        issue(nxt_ref, 1 - slot)

    pltpu.make_async_copy(eo_ref.at[pl.ds(0, CM)], buf_ref.at[slot, 0], sem.at[slot]).wait()
    pltpu.make_async_copy(eo_ref.at[pl.ds(0, CM)], buf_ref.at[slot, 1], sem.at[slot]).wait()
    c0 = route_ref[:, 2:3]
    c1 = route_ref[:, 3:4]
    o_ref[...] = h1_ref[...] + c0 * buf_ref[slot, 0] + c1 * buf_ref[slot, 1]


def _combine(pos01, h1, route, eo):
    n = h1.shape[0]
    nt = n // CM
    return pl.pallas_call(
        _combine_kernel,
        out_shape=jax.ShapeDtypeStruct((n, D_MODEL), f32),
        grid=(nt,),
        in_specs=[pl.BlockSpec((None, 1, 2 * CM), lambda i: (i, 0, 0), memory_space=pltpu.SMEM),
                  pl.BlockSpec((None, 1, 2 * CM), lambda i: (jnp.minimum(i + 1, nt - 1), 0, 0),
                               memory_space=pltpu.SMEM),
                  pl.BlockSpec((CM, D_MODEL), lambda i: (i, 0)),
                  pl.BlockSpec((CM, LANES), lambda i: (i, 0)),
                  pl.BlockSpec(memory_space=pl.ANY)],
        out_specs=pl.BlockSpec((CM, D_MODEL), lambda i: (i, 0)),
        scratch_shapes=[pltpu.VMEM((2, 2, CM, D_MODEL), f32), pltpu.SemaphoreType.DMA((2,))],
        compiler_params=_cparams(("arbitrary",)),
        name="moe_combine",
    )(pos01, pos01, h1, route, eo)


def _route_tables(route, n):
    e = route[:, 0:2].astype(i32).reshape(-1)
    onehot = (e[:, None] == jnp.arange(N_EXPERTS, dtype=i32)[None, :]).astype(i32)
    csum = jnp.cumsum(onehot, axis=0)
    rank = jnp.take_along_axis(csum, e[:, None], axis=1)[:, 0] - 1
    counts = csum[-1]
    ntile = (counts + TR - 1) // TR
    tile_end = jnp.cumsum(ntile)
    tile_start = tile_end - ntile
    pos = tile_start[e] * TR + rank
    s = 2 * n + N_EXPERTS * TR
    slot_token = jnp.zeros((s,), i32).at[pos].set(jnp.arange(2 * n, dtype=i32) // 2)
    n_used = tile_end[-1:]
    t = jnp.minimum(jnp.arange(s // TR, dtype=i32), n_used[0] - 1)
    tile_expert = jnp.sum((tile_end[None, :] <= t[:, None]).astype(i32), axis=1)
    used = jnp.arange(s // TR, dtype=i32) < n_used[0]
    prev = jnp.concatenate([jnp.full((1,), -1, i32), tile_expert[:-1]])
    first = ((tile_expert != prev) & used).astype(i32)
    wslot = (jnp.cumsum(first) - 1) % 2
    nxt_tile = tile_end[tile_expert]
    next_expert = jnp.where((first == 1) & (nxt_tile < n_used[0]),
                            tile_expert[jnp.minimum(nxt_tile, s // TR - 1)], -1)
    pos01 = jnp.concatenate([pos[0::2].reshape(n // CM, 1, CM), pos[1::2].reshape(n // CM, 1, CM)], axis=2)
    return slot_token, tile_expert, n_used.astype(i32), first, wslot.astype(i32), next_expert.astype(i32), pos01


def _pack_w_in(w):
    u, q, k, v, qi, ki, wi, gc, ga = jnp.split(w, [2048, 4096, 4224, 4352, 5376, 5440, 5456, 7504], axis=1)
    pad = jnp.zeros((w.shape[0], LANES - IDX_DIM - IDX_HEADS), w.dtype)
    return jnp.concatenate([u, q, gc, ga, qi, k, v, ki, wi, pad], axis=1).astype(bf16)


def _lane_inv(rot_dim, period):
    inv = ROPE_THETA ** (-jnp.arange(0, rot_dim, 2, dtype=f32) / rot_dim)
    lane = jnp.arange(LANES)
    lp = lane % period
    return jnp.where(lp < rot_dim, inv[lp % (rot_dim // 2)], 0.0).astype(f32)[None, :]


def kernel(x, positions, attn_norm_g, w_in, conv_dw_w, conv_dw_b, conv_ln_g, conv_ln_b, w_conv_out, q_norm_g, k_norm_g, w_attn_o, w_out, ffn_norm_g, w_router_group, b_router_group, w_router_expert, b_router_expert, w_exp_gate, w_exp_up, w_exp_down):
    batch, seq, d = x.shape
    n = batch * seq
    h = x.reshape(n, d)
    pos2 = positions.reshape(n, 1)
    inv_a = _lane_inv(HEAD_DIM // ROT_FRACTION, LANES)
    inv_i = _lane_inv(IDX_DIM // ROT_FRACTION, IDX_DIM)
    for l in range(attn_norm_g.shape[0]):
        z = _inproj(h, attn_norm_g[l][None, :], _pack_w_in(w_in[l]))
        qT, qiT, wT, kr, kir, vT = _prep(z, pos2, inv_a, inv_i, q_norm_g[l][None, :], k_norm_g[l][None, :])
        a = _attention(qT, qiT, wT, kr, kir, vT, batch, seq)
        dw = jnp.concatenate([conv_dw_w[l], jnp.zeros((1, CONV_WIDTH), f32)], axis=0)
        c = _conv(z, dw, conv_dw_b[l][None, :], conv_ln_g[l][None, :], conv_ln_b[l][None, :], batch, seq)
        wr = jnp.concatenate([w_router_group[l], w_router_expert[l],
                              jnp.zeros((d, LANES - N_GROUPS - N_EXPERTS), f32)], axis=1)
        br = jnp.concatenate([b_router_group[l], b_router_expert[l],
                              jnp.zeros((LANES - N_GROUPS - N_EXPERTS,), f32)])[None, :]
        wr_hi = wr.astype(bf16)
        wr2 = jnp.concatenate([wr_hi, (wr - wr_hi.astype(f32)).astype(bf16)], axis=1)
        g2 = ffn_norm_g[l][None, :]
        h1, route = _mix(c, a, z, h, w_conv_out[l].astype(bf16), w_attn_o[l].astype(bf16), w_out[l].astype(bf16),
                         g2, wr2, br)
        slot_token, tile_expert, n_used, first, wslot, next_expert, pos01 = _route_tables(route, n)
        eo = _experts(tile_expert, n_used, first, wslot, next_expert, slot_token, h1, g2,
                      w_exp_gate[l], w_exp_up[l], w_exp_down[l])
        h = _combine(pos01, h1, route, eo)
    return h.reshape(batch, seq, d)
```

```python
import functools

import jax
import jax.numpy as jnp
from jax import lax
from jax.experimental import pallas as pl
from jax.experimental.pallas import tpu as pltpu

f32 = jnp.float32
bf16 = jnp.bfloat16
i32 = jnp.int32

D_MODEL = 2048
CONV_WIDTH = 1024
CONV_KERNEL = 31
N_HEADS = 16
HEAD_DIM = 128
ROPE_THETA = 500000.0
ROT_FRACTION = 4
IDX_HEADS = 16
IDX_DIM = 64
TOPK_MAX = 256
IDX_SCALE = IDX_DIM ** -0.5 * IDX_HEADS ** -0.5
ATTN_SCALE = HEAD_DIM ** -0.5
N_GROUPS = 4
EXPERTS_PER_GROUP = 8
N_EXPERTS = 32
EXPERT_FF = 512
EPS = 1e-6

LANES = 128
SUBLANES = 8
NEG = -1e30
LOG2E = 1.4426950408889634
LOGIT_LIMIT = 86.0
COUNT_CHAINS = 4
SEARCH_INTERP_ITERS = 40
SEARCH_MAX_ITERS = SEARCH_INTERP_ITERS + 34

ZW = 9600
Z_U, Z_Q, Z_GC, Z_GA, Z_QI, Z_K, Z_V, Z_KIW = 0, 2048, 4096, 6144, 8192, 9216, 9344, 9472

QB = 256
KB = 512
VMEM_LIMIT = 56 * 1024 * 1024


def _cparams(sem, flags=None):
    return pltpu.CompilerParams(dimension_semantics=sem, vmem_limit_bytes=VMEM_LIMIT, flags=flags)


def _inproj_kernel(x_ref, g_ref, w_ref, z_ref, hn_ref):
    @pl.when(pl.program_id(1) == 0)
    def _():
        x = x_ref[...]
        ms = jnp.mean(x * x, axis=-1, keepdims=True)
        hn_ref[...] = (x * lax.rsqrt(ms + EPS) * g_ref[...]).astype(bf16)

    z_ref[...] = jnp.dot(hn_ref[...], w_ref[...], preferred_element_type=f32).astype(z_ref.dtype)


def _inproj(x2, g, w_packed, tm=512, tn=1920):
    n = x2.shape[0]
    return pl.pallas_call(
        _inproj_kernel,
        out_shape=jax.ShapeDtypeStruct((n, ZW), bf16),
        grid=(n // tm, ZW // tn),
        in_specs=[pl.BlockSpec((tm, D_MODEL), lambda i, j: (i, 0)),
                  pl.BlockSpec((1, D_MODEL), lambda i, j: (0, 0)),
                  pl.BlockSpec((D_MODEL, tn), lambda i, j: (0, j))],
        out_specs=pl.BlockSpec((tm, tn), lambda i, j: (i, j)),
        scratch_shapes=[pltpu.VMEM((tm, D_MODEL), bf16)],
        compiler_params=_cparams(("parallel", "arbitrary")),
        name="inproj",
    )(x2, g, w_packed)


PT = 512


def _prep_kernel(q_ref, qi_ref, k_ref, v_ref, kiw_ref, pos_ref, inva_ref, invi_ref, qg_ref, kg_ref,
                 qT_ref, qiT_ref, wT_ref, kr_ref, kir_ref, vT_ref):
    lane = lax.broadcasted_iota(i32, (PT, LANES), 1)
    pos = pos_ref[...].astype(f32)
    ang_a = pos * inva_ref[...]
    ca, sa = jnp.cos(ang_a), jnp.sin(ang_a)
    s1a = jnp.where(lane < 16, -sa, 0.0)
    s2a = jnp.where(lane >= 16, sa, 0.0)
    ang_i = pos * invi_ref[...]
    ci, si = jnp.cos(ang_i), jnp.sin(ang_i)
    l64 = lane & 63
    s1i = jnp.where(l64 < 8, -si, 0.0)
    s2i = jnp.where(l64 >= 8, si, 0.0)

    def rope_a(x):
        return x * ca + pltpu.roll(x, 112, 1) * s1a + pltpu.roll(x, 16, 1) * s2a

    def rope_i(x):
        return x * ci + pltpu.roll(x, 120, 1) * s1i + pltpu.roll(x, 8, 1) * s2i

    def rms(x, g):
        ms = jnp.mean(x * x, axis=-1, keepdims=True)
        return x * lax.rsqrt(ms + EPS) * g

    for h in range(N_HEADS):
        qh = q_ref[:, h * HEAD_DIM:(h + 1) * HEAD_DIM].astype(f32)
        qh = rope_a(rms(qh, qg_ref[...])) * (ATTN_SCALE * LOG2E)
        for sb in range(PT // QB):
            qT_ref[sb, :, h * QB:(h + 1) * QB] = qh[sb * QB:(sb + 1) * QB, :].T.astype(bf16)

    for hp in range(IDX_HEADS // 2):
        x = rope_i(qi_ref[:, hp * LANES:(hp + 1) * LANES].astype(f32))
        for sb in range(PT // QB):
            xt = x[sb * QB:(sb + 1) * QB, :].T
            qiT_ref[sb, :, (2 * hp) * QB:(2 * hp + 1) * QB] = xt[0:IDX_DIM].astype(bf16)
            qiT_ref[sb, :, (2 * hp + 1) * QB:(2 * hp + 2) * QB] = xt[IDX_DIM:2 * IDX_DIM].astype(bf16)

    kiw = kiw_ref[...].astype(f32)
    w = pltpu.roll(kiw, 64, 1) * IDX_SCALE
    for sb in range(PT // QB):
        wT_ref[sb] = w[sb * QB:(sb + 1) * QB, :].T[0:IDX_HEADS]

    kir_ref[...] = jnp.where(lane < IDX_DIM, rope_i(kiw), 0.0).astype(bf16)
    kr_ref[...] = rope_a(rms(k_ref[...].astype(f32), kg_ref[...])).astype(bf16)
    vT_ref[0] = v_ref[...].astype(f32).T.astype(bf16)


def _prep(z, pos2, inv_a, inv_i, qg, kg):
    n = z.shape[0]
    nblk = n // QB
    cb = lambda c, w: c // w
    return pl.pallas_call(
        _prep_kernel,
        out_shape=(jax.ShapeDtypeStruct((nblk, HEAD_DIM, N_HEADS * QB), bf16),
                   jax.ShapeDtypeStruct((nblk, IDX_DIM, IDX_HEADS * QB), bf16),
                   jax.ShapeDtypeStruct((nblk, IDX_HEADS, QB), f32),
                   jax.ShapeDtypeStruct((n, HEAD_DIM), bf16),
                   jax.ShapeDtypeStruct((n, LANES), bf16),
                   jax.ShapeDtypeStruct((n // KB, HEAD_DIM, KB), bf16)),
        grid=(n // PT,),
        in_specs=[pl.BlockSpec((PT, 2048), lambda i: (i, cb(Z_Q, 2048))),
                  pl.BlockSpec((PT, 1024), lambda i: (i, cb(Z_QI, 1024))),
                  pl.BlockSpec((PT, LANES), lambda i: (i, cb(Z_K, LANES))),
                  pl.BlockSpec((PT, LANES), lambda i: (i, cb(Z_V, LANES))),
                  pl.BlockSpec((PT, LANES), lambda i: (i, cb(Z_KIW, LANES))),
                  pl.BlockSpec((PT, 1), lambda i: (i, 0)),
                  pl.BlockSpec((1, LANES), lambda i: (0, 0)),
                  pl.BlockSpec((1, LANES), lambda i: (0, 0)),
                  pl.BlockSpec((1, LANES), lambda i: (0, 0)),
                  pl.BlockSpec((1, LANES), lambda i: (0, 0))],
        out_specs=(pl.BlockSpec((PT // QB, HEAD_DIM, N_HEADS * QB), lambda i: (i, 0, 0)),
                   pl.BlockSpec((PT // QB, IDX_DIM, IDX_HEADS * QB), lambda i: (i, 0, 0)),
                   pl.BlockSpec((PT // QB, IDX_HEADS, QB), lambda i: (i, 0, 0)),
                   pl.BlockSpec((PT, HEAD_DIM), lambda i: (i, 0)),
                   pl.BlockSpec((PT, LANES), lambda i: (i, 0)),
                   pl.BlockSpec((1, HEAD_DIM, KB), lambda i: (i, 0, 0))),
        compiler_params=_cparams(("parallel",)),
        name="prep",
    )(z, z, z, z, z, pos2, inv_a, inv_i, qg, kg)


def _f32_to_key(x):
    bits = pltpu.bitcast(x, i32)
    return bits ^ ((bits >> 31) & jnp.int32(0x7FFFFFFF))


def _key_to_f32(k):
    return pltpu.bitcast(k ^ ((k >> 31) & jnp.int32(0x7FFFFFFF)), f32)


def _attn_kernel(qT_ref, qiT_ref, wT_ref, kr_ref, kir_ref, vT_ref, o_ref, sc_ref, acc_ref, m_ref, l_ref, kn_ref):
    qb = pl.program_id(1)
    q0 = qb * QB

    @pl.when(qb == 0)
    def _():
        def kn_body(j, mx):
            kf = kr_ref[pl.ds(pl.multiple_of(j * KB, KB), KB), :].astype(f32)
            return jnp.maximum(mx, jnp.max(jnp.sum(kf * kf, axis=1, keepdims=True), axis=0, keepdims=True))

        kn = lax.fori_loop(0, kr_ref.shape[0] // KB, kn_body, jnp.zeros((1, 1), f32))
        kn_ref[...] = jnp.broadcast_to(kn, kn_ref.shape)

    nkb = (q0 + QB - 1) // KB + 1
    qidx = q0 + lax.broadcasted_iota(i32, (KB, QB), 1)
    row = lax.broadcasted_iota(i32, (KB, QB), 0)

    def score_body(j, carry):
        k0 = pl.multiple_of(j * KB, KB)
        kblk = kir_ref[pl.ds(k0, KB), 0:IDX_DIM]
        sc = jnp.zeros((KB, QB), f32)
        for hp in range(IDX_HEADS // 2):
            s = jnp.dot(kblk, qiT_ref[:, hp * 2 * QB:(hp + 1) * 2 * QB], preferred_element_type=f32)
            s = jnp.maximum(s, 0.0)
            sc = sc + s[:, 0:QB] * wT_ref[2 * hp:2 * hp + 1, :] + s[:, QB:2 * QB] * wT_ref[2 * hp + 1:2 * hp + 2, :]
        causal = k0 + row <= qidx
        sc_ref[pl.ds(k0, KB), :] = jnp.where(causal, sc, -jnp.inf)
        mn, mx = carry
        mn = jnp.minimum(mn, jnp.min(jnp.where(causal, sc, jnp.inf).reshape(KB // 8, 8, QB), axis=0))
        mx = jnp.maximum(mx, jnp.max(jnp.where(causal, sc, -jnp.inf).reshape(KB // 8, 8, QB), axis=0))
        return mn, mx

    mn8, mx8 = lax.fori_loop(0, nkb, score_body,
                             (jnp.full((8, QB), jnp.inf, f32), jnp.full((8, QB), -jnp.inf, f32)))

    lane1 = lax.broadcasted_iota(i32, (1, QB), 1)
    kt = jnp.minimum(TOPK_MAX, q0 + lane1 + 1)

    def count_ge(cand):
        candf = _key_to_f32(cand)

        rows = KB // COUNT_CHAINS

        def cnt_body(j, cs):
            k0 = pl.multiple_of(j * KB, KB)
            out = []
            for i, c in enumerate(cs):
                ge = (sc_ref[pl.ds(k0 + i * rows, rows), :] >= candf).astype(i32)
                out.append(c + jnp.sum(ge.reshape(rows // 8, 8, QB), axis=0))
            return tuple(out)

        cs = lax.fori_loop(0, nkb, cnt_body, tuple(jnp.zeros((8, QB), i32) for _ in range(COUNT_CHAINS)))
        c8 = (cs[0] + cs[1]) + (cs[2] + cs[3])
        return jnp.sum(c8, axis=0, keepdims=True)

    def active(lo, hi, clo):
        return (clo != kt) & (lo + 1 < hi)

    def search_cond(st):
        it, lo, hi, clo, chi, flo, fhi, side = st
        return (it < SEARCH_MAX_ITERS) & (jnp.sum(active(lo, hi, clo).astype(f32)) > 0.0)

    def search_step(st):
        it, lo, hi, clo, chi, flo, fhi, side = st
        act = active(lo, hi, clo)
        lov = _key_to_f32(lo)
        hiv = _key_to_f32(hi - 1)
        guess = _f32_to_key(lov + (hiv - lov) * (flo / (flo - fhi)))
        mid = (lo >> 1) + (hi >> 1) + (lo & hi & 1)
        cand = jnp.where(it < SEARCH_INTERP_ITERS, guess, mid)
        cand = jnp.minimum(jnp.maximum(cand, lo + 1), hi - 1)
        cnt = count_ge(cand)
        up = act & (cnt >= kt)
        dn = act & (cnt < kt)
        f_new = (cnt - kt).astype(f32)
        flo2 = jnp.where(up, f_new + 0.5, jnp.where(dn & (side < 0), flo * 0.5, flo))
        fhi2 = jnp.where(dn, f_new - 0.5, jnp.where(up & (side > 0), fhi * 0.5, fhi))
        side2 = jnp.where(up, 1, jnp.where(dn, -1, side))
        return (it + 1, jnp.where(up, cand, lo), jnp.where(dn, cand, hi), jnp.where(up, cnt, clo),
                jnp.where(dn, cnt, chi), flo2, fhi2, side2)

    lo0 = _f32_to_key(jnp.min(mn8, axis=0, keepdims=True))
    hi0 = _f32_to_key(jnp.max(mx8, axis=0, keepdims=True)) + 1
    clo0 = q0 + lane1 + 1
    chi0 = jnp.zeros((1, QB), i32)
    st = (jnp.int32(0), lo0, hi0, clo0, chi0, (clo0 - kt).astype(f32) + 0.5, (chi0 - kt).astype(f32) - 0.5,
          jnp.zeros((1, QB), i32))
    thr = lax.while_loop(search_cond, lambda s: search_step(search_step(s)), st)[1]
    thr_f = jnp.where(clo0 <= TOPK_MAX, -jnp.inf, _key_to_f32(thr))

    l_ref[...] = jnp.zeros(l_ref.shape, f32)
    acc_ref[...] = jnp.zeros(acc_ref.shape, f32)

    def block_bias(j):
        k0 = pl.multiple_of(j * KB, KB)
        sel = (sc_ref[pl.ds(k0, KB), :] >= thr_f) & (k0 + row <= qidx)
        bias1 = jnp.where(sel, 0.0, NEG).astype(f32)
        return k0, jnp.concatenate([bias1] * N_HEADS, axis=1)

    qf = qT_ref[...].astype(f32)
    qn2 = jnp.max(jnp.sum(qf * qf, axis=0, keepdims=True), axis=1, keepdims=True)
    bounded = jnp.max(qn2 * kn_ref[0:1, 0:1]) <= LOGIT_LIMIT * LOGIT_LIMIT

    def att_bounded(j, carry):
        k0, bias = block_bias(j)
        kblk = kr_ref[pl.ds(k0, KB), :]
        vt = vT_ref[j]
        p = jnp.exp2(jnp.dot(kblk, qT_ref[...], preferred_element_type=f32) + bias)
        l_ref[...] += jnp.sum(p, axis=0, keepdims=True)
        acc_ref[...] += jnp.dot(vt, p.astype(bf16), preferred_element_type=f32)
        return carry

    def att_online(j, carry):
        k0, bias = block_bias(j)
        kblk = kr_ref[pl.ds(k0, KB), :]
        vt = vT_ref[j]
        s = jnp.dot(kblk, qT_ref[...], preferred_element_type=f32) + bias
        m_old = m_ref[...]
        m_new = jnp.maximum(m_old, jnp.max(s, axis=0, keepdims=True))
        p = jnp.exp2(s - m_new)
        alpha = jnp.exp2(m_old - m_new)
        l_ref[...] = l_ref[...] * alpha + jnp.sum(p, axis=0, keepdims=True)
        m_ref[...] = m_new
        acc_ref[...] = acc_ref[...] * alpha + jnp.dot(vt, p.astype(bf16), preferred_element_type=f32)
        return carry

    @pl.when(bounded)
    def _():
        lax.fori_loop(0, nkb, att_bounded, 0)

    @pl.when(jnp.logical_not(bounded))
    def _():
        m_ref[...] = jnp.full(m_ref.shape, NEG, f32)
        lax.fori_loop(0, nkb, att_online, 0)

    out_t = acc_ref[...] / l_ref[...]
    for h in range(N_HEADS):
        o_ref[:, h * HEAD_DIM:(h + 1) * HEAD_DIM] = out_t[:, h * QB:(h + 1) * QB].T.astype(o_ref.dtype)


def _attention(qT, qiT, wT, kr, kir, vT, batch, seq):
    nq = seq // QB
    n = batch * seq
    return pl.pallas_call(
        _attn_kernel,
        out_shape=jax.ShapeDtypeStruct((n, N_HEADS * HEAD_DIM), bf16),
        grid=(batch, nq),
        in_specs=[pl.BlockSpec((None, HEAD_DIM, N_HEADS * QB), lambda b, i: (b * nq + i, 0, 0)),
                  pl.BlockSpec((None, IDX_DIM, IDX_HEADS * QB), lambda b, i: (b * nq + i, 0, 0)),
                  pl.BlockSpec((None, IDX_HEADS, QB), lambda b, i: (b * nq + i, 0, 0)),
                  pl.BlockSpec((seq, HEAD_DIM), lambda b, i: (b, 0)),
                  pl.BlockSpec((seq, LANES), lambda b, i: (b, 0)),
                  pl.BlockSpec((seq // KB, HEAD_DIM, KB), lambda b, i: (b, 0, 0))],
        out_specs=pl.BlockSpec((QB, N_HEADS * HEAD_DIM), lambda b, i: (b * nq + i, 0)),
        scratch_shapes=[pltpu.VMEM((seq, QB), f32),
                        pltpu.VMEM((HEAD_DIM, N_HEADS * QB), f32),
                        pltpu.VMEM((1, N_HEADS * QB), f32),
                        pltpu.VMEM((1, N_HEADS * QB), f32),
                        pltpu.VMEM((8, LANES), f32)],
        compiler_params=_cparams(("parallel", "arbitrary")),
        name="attn",
    )(qT, qiT, wT, kr, kir, vT)


CT = 256
HALO = 32


def _conv_kernel(u_ref, w_ref, b_ref, g_ref, beta_ref, c_ref, ybuf_ref, cbuf_ref):
    i = pl.program_id(1)

    @pl.when(i == 0)
    def _():
        ybuf_ref[0:HALO, :] = jnp.zeros((HALO, CONV_WIDTH), f32)

    @pl.when(i > 0)
    def _():
        ybuf_ref[0:HALO, :] = ybuf_ref[CT:CT + HALO, :]

    a = u_ref[:, 0:CONV_WIDTH].astype(f32)
    gate = u_ref[:, CONV_WIDTH:2 * CONV_WIDTH].astype(f32)
    ybuf_ref[HALO:HALO + CT, :] = a * jax.nn.sigmoid(gate)

    off = HALO - (CONV_KERNEL - 1)
    for c in range(CONV_WIDTH // LANES):
        cs = slice(c * LANES, (c + 1) * LANES)
        acc = jnp.broadcast_to(b_ref[:, cs], (CT, LANES))
        for ph in range(SUBLANES):
            rows = CT if ph == 0 else CT + SUBLANES
            part = None
            for o in range(ph, off + CONV_KERNEL, SUBLANES):
                if o < off:
                    continue
                term = w_ref[o - off:o - off + 1, cs] * ybuf_ref[o - ph:o - ph + rows, cs]
                part = term if part is None else part + term
            acc = acc + part[ph:ph + CT]
        cbuf_ref[:, cs] = acc

    y = cbuf_ref[...]
    mu = jnp.mean(y, axis=-1, keepdims=True)
    yc = y - mu
    var = jnp.mean(yc * yc, axis=-1, keepdims=True)
    yn = yc * lax.rsqrt(var + EPS) * g_ref[...] + beta_ref[...]
    c_ref[...] = (yn * jax.nn.sigmoid(yn)).astype(c_ref.dtype)


def _conv(z, dw_w, dw_b, ln_g, ln_b, batch, seq):
    nt = seq // CT
    n = batch * seq
    return pl.pallas_call(
        _conv_kernel,
        out_shape=jax.ShapeDtypeStruct((n, CONV_WIDTH), bf16),
        grid=(batch, nt),
        in_specs=[pl.BlockSpec((CT, 2 * CONV_WIDTH), lambda b, i: (b * nt + i, Z_U // 2048)),
                  pl.BlockSpec((32, CONV_WIDTH), lambda b, i: (0, 0)),
                  pl.BlockSpec((1, CONV_WIDTH), lambda b, i: (0, 0)),
                  pl.BlockSpec((1, CONV_WIDTH), lambda b, i: (0, 0)),
                  pl.BlockSpec((1, CONV_WIDTH), lambda b, i: (0, 0))],
        out_specs=pl.BlockSpec((CT, CONV_WIDTH), lambda b, i: (b * nt + i, 0)),
        scratch_shapes=[pltpu.VMEM((CT + HALO, CONV_WIDTH), f32), pltpu.VMEM((CT, CONV_WIDTH), f32)],
        compiler_params=_cparams(("parallel", "arbitrary")),
        name="conv",
    )(z, dw_w, dw_b, ln_g, ln_b)


MT = 256


def _mix_kernel(c_ref, a_ref, gc_ref, ga_ref, x_ref, wco_ref, wao_ref, wout_ref, g2_ref, wr_ref, br_ref,
                h1_ref, route_ref):
    yc = jnp.dot(c_ref[...], wco_ref[...], preferred_element_type=f32)
    ya = jnp.dot(a_ref[...], wao_ref[...], preferred_element_type=f32)
    mix = jax.nn.sigmoid(gc_ref[...].astype(f32)) * yc + jax.nn.sigmoid(ga_ref[...].astype(f32)) * ya
    h1 = x_ref[...] + jnp.dot(mix.astype(bf16), wout_ref[...], preferred_element_type=f32)
    h1_ref[...] = h1

    ms = jnp.mean(h1 * h1, axis=-1, keepdims=True)
    hn = h1 * lax.rsqrt(ms + EPS) * g2_ref[...]
    hn_hi = hn.astype(bf16)
    hn_lo = (hn - hn_hi.astype(f32)).astype(bf16)
    r_hi = jnp.dot(hn_hi, wr_ref[...], preferred_element_type=f32)
    r_lo = jnp.dot(hn_lo, wr_ref[:, 0:LANES], preferred_element_type=f32)
    lg = r_hi[:, 0:LANES] + (r_hi[:, LANES:2 * LANES] + r_lo) + br_ref[...]
    lane = lax.broadcasted_iota(i32, lg.shape, 1)
    big = jnp.int32(1 << 20)

    gmask = lane < N_GROUPS
    gl = jnp.where(gmask, lg, -jnp.inf)
    gmax = jnp.max(gl, axis=-1, keepdims=True)
    gidx = jnp.min(jnp.where(gl == gmax, lane, big), axis=-1, keepdims=True)
    pg = 1.0 / jnp.sum(jnp.where(gmask, jnp.exp(gl - gmax), 0.0), axis=-1, keepdims=True)

    lo = N_GROUPS + EXPERTS_PER_GROUP * gidx
    emask = (lane >= lo) & (lane < lo + EXPERTS_PER_GROUP)
    el = jnp.where(emask, lg, -jnp.inf)
    m0 = jnp.max(el, axis=-1, keepdims=True)
    i0 = jnp.min(jnp.where(el == m0, lane, big), axis=-1, keepdims=True)
    el1 = jnp.where(lane == i0, -jnp.inf, el)
    m1 = jnp.max(el1, axis=-1, keepdims=True)
    i1 = jnp.min(jnp.where(el1 == m1, lane, big), axis=-1, keepdims=True)
    e1x = jnp.exp(m1 - m0)
    w0 = 1.0 / (1.0 + e1x)
    w1 = e1x * w0

    route = jnp.where(lane == 0, (i0 - N_GROUPS).astype(f32), 0.0)
    route = jnp.where(lane == 1, (i1 - N_GROUPS).astype(f32), route)
    route = jnp.where(lane == 2, pg * w0, route)
    route = jnp.where(lane == 3, pg * w1, route)
    route_ref[...] = route


def _mix(c, a, z, x2, wco, wao, wout, g2, wr, br):
    n = x2.shape[0]
    once = functools.partial(pl.BlockSpec, pipeline_mode=pl.Buffered(1))
    return pl.pallas_call(
        _mix_kernel,
        out_shape=(jax.ShapeDtypeStruct((n, D_MODEL), f32), jax.ShapeDtypeStruct((n, LANES), f32)),
        grid=(n // MT,),
        in_specs=[pl.BlockSpec((MT, CONV_WIDTH), lambda i: (i, 0)),
                  pl.BlockSpec((MT, D_MODEL), lambda i: (i, 0)),
                  pl.BlockSpec((MT, D_MODEL), lambda i: (i, Z_GC // 2048)),
                  pl.BlockSpec((MT, D_MODEL), lambda i: (i, Z_GA // 2048)),
                  pl.BlockSpec((MT, D_MODEL), lambda i: (i, 0)),
                  once((CONV_WIDTH, D_MODEL), lambda i: (0, 0)),
                  once((D_MODEL, D_MODEL), lambda i: (0, 0)),
                  once((D_MODEL, D_MODEL), lambda i: (0, 0)),
                  pl.BlockSpec((1, D_MODEL), lambda i: (0, 0)),
                  once((D_MODEL, 2 * LANES), lambda i: (0, 0)),
                  pl.BlockSpec((1, LANES), lambda i: (0, 0))],
        out_specs=(pl.BlockSpec((MT, D_MODEL), lambda i: (i, 0)), pl.BlockSpec((MT, LANES), lambda i: (i, 0))),
        compiler_params=_cparams(("parallel",)),
        name="mix",
    )(c, a, z, z, x2, wco, wao, wout, g2, wr, br)


TR = 256
CM = 256


def _expert_kernel(te_ref, nu_ref, first_ref, wsl_ref, nxe_ref, cur_ref, nxt_ref, h_ref, g2_ref,
                   wg_hbm, wu_hbm, wd_hbm, o_ref,
                   xbuf_ref, wgf_ref, wuf_ref, wdf_ref, wgb_ref, wub_ref, wdb_ref, sem, wsem):
    t = pl.program_id(0)
    nu = nu_ref[0]
    slot = lax.rem(t, 2)

    def issue(idx_ref, dst_slot):
        def body(r, carry):
            pltpu.make_async_copy(h_ref.at[pl.ds(idx_ref[0, r], 1)], xbuf_ref.at[dst_slot, pl.ds(r, 1)],
                                  sem.at[dst_slot]).start()
            return carry

        lax.fori_loop(0, TR, body, 0, unroll=8)

    def weight_copies(e, ws):
        return (pltpu.make_async_copy(wg_hbm.at[e], wgf_ref.at[ws], wsem.at[ws]),
                pltpu.make_async_copy(wu_hbm.at[e], wuf_ref.at[ws], wsem.at[ws]),
                pltpu.make_async_copy(wd_hbm.at[e], wdf_ref.at[ws], wsem.at[ws]))

    @pl.when(t == 0)
    def _():
        for cp in weight_copies(te_ref[0], 0):
            cp.start(priority=1)
        issue(cur_ref, 0)

    @pl.when(t + 1 < nu)
    def _():
        issue(nxt_ref, 1 - slot)

    @pl.when(t < nu)
    def _():
        @pl.when(first_ref[t] == 1)
        def _():
            ws = wsl_ref[t]
            for cp in weight_copies(te_ref[t], ws):
                cp.wait()

            @pl.when(nxe_ref[t] >= 0)
            def _():
                for cp in weight_copies(nxe_ref[t], 1 - ws):
                    cp.start(priority=1)

            wgb_ref[...] = wgf_ref[ws].astype(bf16)
            wub_ref[...] = wuf_ref[ws].astype(bf16)
            wdb_ref[...] = wdf_ref[ws].astype(bf16)

        pltpu.make_async_copy(h_ref.at[pl.ds(0, TR)], xbuf_ref.at[slot], sem.at[slot]).wait()
        x = xbuf_ref[slot]
        ms = jnp.mean(x * x, axis=-1, keepdims=True)
        hn = (x * lax.rsqrt(ms + EPS) * g2_ref[...]).astype(bf16)
        gate = jnp.dot(hn, wgb_ref[...], preferred_element_type=f32)
        up = jnp.dot(hn, wub_ref[...], preferred_element_type=f32)
        act = (gate * jax.nn.sigmoid(gate) * up).astype(bf16)
        o_ref[...] = jnp.dot(act, wdb_ref[...], preferred_element_type=f32)

    @pl.when(t >= nu)
    def _():
        o_ref[...] = jnp.zeros(o_ref.shape, o_ref.dtype)


def _experts(tile_expert, n_used, first, wslot, next_expert, slot_token, h1, g2, wg, wu, wd):
    s = slot_token.shape[0]
    ntiles = s // TR
    idx3 = slot_token.reshape(ntiles, 1, TR)
    return pl.pallas_call(
        _expert_kernel,
        out_shape=jax.ShapeDtypeStruct((s, D_MODEL), f32),
        grid_spec=pltpu.PrefetchScalarGridSpec(
            num_scalar_prefetch=5,
            grid=(ntiles,),
            in_specs=[pl.BlockSpec((None, 1, TR), lambda t, *_: (t, 0, 0), memory_space=pltpu.SMEM),
                      pl.BlockSpec((None, 1, TR), lambda t, *_: (jnp.minimum(t + 1, ntiles - 1), 0, 0),
                                   memory_space=pltpu.SMEM),
                      pl.BlockSpec(memory_space=pl.ANY),
                      pl.BlockSpec((1, D_MODEL), lambda t, *_: (0, 0)),
                      pl.BlockSpec(memory_space=pl.ANY),
                      pl.BlockSpec(memory_space=pl.ANY),
                      pl.BlockSpec(memory_space=pl.ANY)],
            out_specs=pl.BlockSpec((TR, D_MODEL), lambda t, *_: (t, 0)),
            scratch_shapes=[pltpu.VMEM((2, TR, D_MODEL), f32),
                            pltpu.VMEM((2, D_MODEL, EXPERT_FF), f32),
                            pltpu.VMEM((2, D_MODEL, EXPERT_FF), f32),
                            pltpu.VMEM((2, EXPERT_FF, D_MODEL), f32),
                            pltpu.VMEM((D_MODEL, EXPERT_FF), bf16),
                            pltpu.VMEM((D_MODEL, EXPERT_FF), bf16),
                            pltpu.VMEM((EXPERT_FF, D_MODEL), bf16),
                            pltpu.SemaphoreType.DMA((2,)),
                            pltpu.SemaphoreType.DMA((2,))]),
        compiler_params=_cparams(("arbitrary",)),
        name="moe_experts",
    )(tile_expert, n_used, first, wslot, next_expert, idx3, idx3, h1, g2, wg, wu, wd)


def _combine_kernel(cur_ref, nxt_ref, h1_ref, route_ref, eo_ref, o_ref, buf_ref, sem):
    i = pl.program_id(0)
    slot = lax.rem(i, 2)

    def issue(idx_ref, dst_slot):
        def body(r, carry):
            pltpu.make_async_copy(eo_ref.at[pl.ds(idx_ref[0, r], 1)], buf_ref.at[dst_slot, 0, pl.ds(r, 1)],
                                  sem.at[dst_slot]).start()
            pltpu.make_async_copy(eo_ref.at[pl.ds(idx_ref[0, CM + r], 1)], buf_ref.at[dst_slot, 1, pl.ds(r, 1)],
                                  sem.at[dst_slot]).start()
            return carry

        lax.fori_loop(0, CM, body, 0, unroll=8)

    @pl.when(i == 0)
    def _():
        issue(cur_ref, 0)

    @pl.when(i + 1 < pl.num_programs(0))
    def _():
        issue(nxt_ref, 1 - slot)

    pltpu.make_async_copy(eo_ref.at[pl.ds(0, CM)], buf_ref.at[slot, 0], sem.at[slot]).wait()
    pltpu.make_async_copy(eo_ref.at[pl.ds(0, CM)], buf_ref.at[slot, 1], sem.at[slot]).wait()
    c0 = route_ref[:, 2:3]
    c1 = route_ref[:, 3:4]
    o_ref[...] = h1_ref[...] + c0 * buf_ref[slot, 0] + c1 * buf_ref[slot, 1]


def _combine(pos01, h1, route, eo):
    n = h1.shape[0]
    nt = n // CM
    return pl.pallas_call(
        _combine_kernel,
        out_shape=jax.ShapeDtypeStruct((n, D_MODEL), f32),
        grid=(nt,),
        in_specs=[pl.BlockSpec((None, 1, 2 * CM), lambda i: (i, 0, 0), memory_space=pltpu.SMEM),
                  pl.BlockSpec((None, 1, 2 * CM), lambda i: (jnp.minimum(i + 1, nt - 1), 0, 0),
                               memory_space=pltpu.SMEM),
                  pl.BlockSpec((CM, D_MODEL), lambda i: (i, 0)),
                  pl.BlockSpec((CM, LANES), lambda i: (i, 0)),
                  pl.BlockSpec(memory_space=pl.ANY)],
        out_specs=pl.BlockSpec((CM, D_MODEL), lambda i: (i, 0)),
        scratch_shapes=[pltpu.VMEM((2, 2, CM, D_MODEL), f32), pltpu.SemaphoreType.DMA((2,))],
        compiler_params=_cparams(("arbitrary",)),
        name="moe_combine",
    )(pos01, pos01, h1, route, eo)


def _route_tables(route, n):
    e = route[:, 0:2].astype(i32).reshape(-1)
    onehot = (e[:, None] == jnp.arange(N_EXPERTS, dtype=i32)[None, :]).astype(i32)
    csum = jnp.cumsum(onehot, axis=0)
    rank = jnp.take_along_axis(csum, e[:, None], axis=1)[:, 0] - 1
    counts = csum[-1]
    ntile = (counts + TR - 1) // TR
    tile_end = jnp.cumsum(ntile)
    tile_start = tile_end - ntile
    pos = tile_start[e] * TR + rank
    s = 2 * n + N_EXPERTS * TR
    slot_token = jnp.zeros((s,), i32).at[pos].set(jnp.arange(2 * n, dtype=i32) // 2)
    n_used = tile_end[-1:]
    t = jnp.minimum(jnp.arange(s // TR, dtype=i32), n_used[0] - 1)
    tile_expert = jnp.sum((tile_end[None, :] <= t[:, None]).astype(i32), axis=1)
    used = jnp.arange(s // TR, dtype=i32) < n_used[0]
    prev = jnp.concatenate([jnp.full((1,), -1, i32), tile_expert[:-1]])
    first = ((tile_expert != prev) & used).astype(i32)
    wslot = (jnp.cumsum(first) - 1) % 2
    nxt_tile = tile_end[tile_expert]
    next_expert = jnp.where((first == 1) & (nxt_tile < n_used[0]),
                            tile_expert[jnp.minimum(nxt_tile, s // TR - 1)], -1)
    pos01 = jnp.concatenate([pos[0::2].reshape(n // CM, 1, CM), pos[1::2].reshape(n // CM, 1, CM)], axis=2)
    return slot_token, tile_expert, n_used.astype(i32), first, wslot.astype(i32), next_expert.astype(i32), pos01


def _pack_w_in(w):
    u, q, k, v, qi, ki, wi, gc, ga = jnp.split(w, [2048, 4096, 4224, 4352, 5376, 5440, 5456, 7504], axis=1)
    pad = jnp.zeros((w.shape[0], LANES - IDX_DIM - IDX_HEADS), w.dtype)
    return jnp.concatenate([u, q, gc, ga, qi, k, v, ki, wi, pad], axis=1).astype(bf16)


def _lane_inv(rot_dim, period):
    inv = ROPE_THETA ** (-jnp.arange(0, rot_dim, 2, dtype=f32) / rot_dim)
    lane = jnp.arange(LANES)
    lp = lane % period
    return jnp.where(lp < rot_dim, inv[lp % (rot_dim // 2)], 0.0).astype(f32)[None, :]


def kernel(x, positions, attn_norm_g, w_in, conv_dw_w, conv_dw_b, conv_ln_g, conv_ln_b, w_conv_out, q_norm_g, k_norm_g, w_attn_o, w_out, ffn_norm_g, w_router_group, b_router_group, w_router_expert, b_router_expert, w_exp_gate, w_exp_up, w_exp_down):
    batch, seq, d = x.shape
    n = batch * seq
    h = x.reshape(n, d)
    pos2 = positions.reshape(n, 1)
    inv_a = _lane_inv(HEAD_DIM // ROT_FRACTION, LANES)
    inv_i = _lane_inv(IDX_DIM // ROT_FRACTION, IDX_DIM)
    for l in range(attn_norm_g.shape[0]):
        z = _inproj(h, attn_norm_g[l][None, :], _pack_w_in(w_in[l]))
        qT, qiT, wT, kr, kir, vT = _prep(z, pos2, inv_a, inv_i, q_norm_g[l][None, :], k_norm_g[l][None, :])
        a = _attention(qT, qiT, wT, kr, kir, vT, batch, seq)
        dw = jnp.concatenate([conv_dw_w[l], jnp.zeros((1, CONV_WIDTH), f32)], axis=0)
        c = _conv(z, dw, conv_dw_b[l][None, :], conv_ln_g[l][None, :], conv_ln_b[l][None, :], batch, seq)
        wr = jnp.concatenate([w_router_group[l], w_router_expert[l],
                              jnp.zeros((d, LANES - N_GROUPS - N_EXPERTS), f32)], axis=1)
        br = jnp.concatenate([b_router_group[l], b_router_expert[l],
                              jnp.zeros((LANES - N_GROUPS - N_EXPERTS,), f32)])[None, :]
        wr_hi = wr.astype(bf16)
        wr2 = jnp.concatenate([wr_hi, (wr - wr_hi.astype(f32)).astype(bf16)], axis=1)
        g2 = ffn_norm_g[l][None, :]
        h1, route = _mix(c, a, z, h, w_conv_out[l].astype(bf16), w_attn_o[l].astype(bf16), w_out[l].astype(bf16),
                         g2, wr2, br)
        slot_token, tile_expert, n_used, first, wslot, next_expert, pos01 = _route_tables(route, n)
        eo = _experts(tile_expert, n_used, first, wslot, next_expert, slot_token, h1, g2,
                      w_exp_gate[l], w_exp_up[l], w_exp_down[l])
        h = _combine(pos01, h1, route, eo)
    return h.reshape(batch, seq, d)
```

```python
import functools

import jax
import jax.numpy as jnp
from jax import lax
from jax.experimental import pallas as pl
from jax.experimental.pallas import tpu as pltpu

f32 = jnp.float32
bf16 = jnp.bfloat16
i32 = jnp.int32

D_MODEL = 2048
CONV_WIDTH = 1024
CONV_KERNEL = 31
N_HEADS = 16
HEAD_DIM = 128
ROPE_THETA = 500000.0
ROT_FRACTION = 4
IDX_HEADS = 16
IDX_DIM = 64
TOPK_MAX = 256
IDX_SCALE = IDX_DIM ** -0.5 * IDX_HEADS ** -0.5
ATTN_SCALE = HEAD_DIM ** -0.5
N_GROUPS = 4
EXPERTS_PER_GROUP = 8
N_EXPERTS = 32
EXPERT_FF = 512
EPS = 1e-6

LANES = 128
SUBLANES = 8
NEG = -1e30
LOG2E = 1.4426950408889634
LOGIT_LIMIT = 86.0
COUNT_CHAINS = 4
SEARCH_INTERP_ITERS = 40
SEARCH_MAX_ITERS = SEARCH_INTERP_ITERS + 34

ZW = 9600
Z_U, Z_Q, Z_GC, Z_GA, Z_QI, Z_K, Z_V, Z_KIW = 0, 2048, 4096, 6144, 8192, 9216, 9344, 9472

QB = 256
KB = 512
VMEM_LIMIT = 56 * 1024 * 1024


def _cparams(sem, flags=None):
    return pltpu.CompilerParams(dimension_semantics=sem, vmem_limit_bytes=VMEM_LIMIT, flags=flags)


def _inproj_kernel(x_ref, g_ref, w_ref, z_ref, hn_ref):
    @pl.when(pl.program_id(1) == 0)
    def _():
        x = x_ref[...]
        ms = jnp.mean(x * x, axis=-1, keepdims=True)
        hn_ref[...] = (x * lax.rsqrt(ms + EPS) * g_ref[...]).astype(bf16)

    z_ref[...] = jnp.dot(hn_ref[...], w_ref[...], preferred_element_type=f32).astype(z_ref.dtype)


def _inproj(x2, g, w_packed, tm=512, tn=1920):
    n = x2.shape[0]
    return pl.pallas_call(
        _inproj_kernel,
        out_shape=jax.ShapeDtypeStruct((n, ZW), bf16),
        grid=(n // tm, ZW // tn),
        in_specs=[pl.BlockSpec((tm, D_MODEL), lambda i, j: (i, 0)),
                  pl.BlockSpec((1, D_MODEL), lambda i, j: (0, 0)),
                  pl.BlockSpec((D_MODEL, tn), lambda i, j: (0, j))],
        out_specs=pl.BlockSpec((tm, tn), lambda i, j: (i, j)),
        scratch_shapes=[pltpu.VMEM((tm, D_MODEL), bf16)],
        compiler_params=_cparams(("parallel", "arbitrary")),
        name="inproj",
    )(x2, g, w_packed)


PT = 512


def _prep_kernel(q_ref, qi_ref, k_ref, v_ref, kiw_ref, pos_ref, inva_ref, invi_ref, qg_ref, kg_ref,
                 qT_ref, qiT_ref, wT_ref, kr_ref, kir_ref, vT_ref):
    lane = lax.broadcasted_iota(i32, (PT, LANES), 1)
    pos = pos_ref[...].astype(f32)
    ang_a = pos * inva_ref[...]
    ca, sa = jnp.cos(ang_a), jnp.sin(ang_a)
    s1a = jnp.where(lane < 16, -sa, 0.0)
    s2a = jnp.where(lane >= 16, sa, 0.0)
    ang_i = pos * invi_ref[...]
    ci, si = jnp.cos(ang_i), jnp.sin(ang_i)
    l64 = lane & 63
    s1i = jnp.where(l64 < 8, -si, 0.0)
    s2i = jnp.where(l64 >= 8, si, 0.0)

    def rope_a(x):
        return x * ca + pltpu.roll(x, 112, 1) * s1a + pltpu.roll(x, 16, 1) * s2a

    def rope_i(x):
        return x * ci + pltpu.roll(x, 120, 1) * s1i + pltpu.roll(x, 8, 1) * s2i

    def rms(x, g):
        ms = jnp.mean(x * x, axis=-1, keepdims=True)
        return x * lax.rsqrt(ms + EPS) * g

    for h in range(N_HEADS):
        qh = q_ref[:, h * HEAD_DIM:(h + 1) * HEAD_DIM].astype(f32)
        qh = rope_a(rms(qh, qg_ref[...])) * (ATTN_SCALE * LOG2E)
        for sb in range(PT // QB):
            qT_ref[sb, :, h * QB:(h + 1) * QB] = qh[sb * QB:(sb + 1) * QB, :].T.astype(bf16)

    for hp in range(IDX_HEADS // 2):
        x = rope_i(qi_ref[:, hp * LANES:(hp + 1) * LANES].astype(f32))
        for sb in range(PT // QB):
            xt = x[sb * QB:(sb + 1) * QB, :].T
            qiT_ref[sb, :, (2 * hp) * QB:(2 * hp + 1) * QB] = xt[0:IDX_DIM].astype(bf16)
            qiT_ref[sb, :, (2 * hp + 1) * QB:(2 * hp + 2) * QB] = xt[IDX_DIM:2 * IDX_DIM].astype(bf16)

    kiw = kiw_ref[...].astype(f32)
    w = pltpu.roll(kiw, 64, 1) * IDX_SCALE
    for sb in range(PT // QB):
        wT_ref[sb] = w[sb * QB:(sb + 1) * QB, :].T[0:IDX_HEADS]

    kir_ref[...] = jnp.where(lane < IDX_DIM, rope_i(kiw), 0.0).astype(bf16)
    kr_ref[...] = rope_a(rms(k_ref[...].astype(f32), kg_ref[...])).astype(bf16)
    vT_ref[0] = v_ref[...].astype(f32).T.astype(bf16)


def _prep(z, pos2, inv_a, inv_i, qg, kg):
    n = z.shape[0]
    nblk = n // QB
    cb = lambda c, w: c // w
    return pl.pallas_call(
        _prep_kernel,
        out_shape=(jax.ShapeDtypeStruct((nblk, HEAD_DIM, N_HEADS * QB), bf16),
                   jax.ShapeDtypeStruct((nblk, IDX_DIM, IDX_HEADS * QB), bf16),
                   jax.ShapeDtypeStruct((nblk, IDX_HEADS, QB), f32),
                   jax.ShapeDtypeStruct((n, HEAD_DIM), bf16),
                   jax.ShapeDtypeStruct((n, LANES), bf16),
                   jax.ShapeDtypeStruct((n // KB, HEAD_DIM, KB), bf16)),
        grid=(n // PT,),
        in_specs=[pl.BlockSpec((PT, 2048), lambda i: (i, cb(Z_Q, 2048))),
                  pl.BlockSpec((PT, 1024), lambda i: (i, cb(Z_QI, 1024))),
                  pl.BlockSpec((PT, LANES), lambda i: (i, cb(Z_K, LANES))),
                  pl.BlockSpec((PT, LANES), lambda i: (i, cb(Z_V, LANES))),
                  pl.BlockSpec((PT, LANES), lambda i: (i, cb(Z_KIW, LANES))),
                  pl.BlockSpec((PT, 1), lambda i: (i, 0)),
                  pl.BlockSpec((1, LANES), lambda i: (0, 0)),
                  pl.BlockSpec((1, LANES), lambda i: (0, 0)),
                  pl.BlockSpec((1, LANES), lambda i: (0, 0)),
                  pl.BlockSpec((1, LANES), lambda i: (0, 0))],
        out_specs=(pl.BlockSpec((PT // QB, HEAD_DIM, N_HEADS * QB), lambda i: (i, 0, 0)),
                   pl.BlockSpec((PT // QB, IDX_DIM, IDX_HEADS * QB), lambda i: (i, 0, 0)),
                   pl.BlockSpec((PT // QB, IDX_HEADS, QB), lambda i: (i, 0, 0)),
                   pl.BlockSpec((PT, HEAD_DIM), lambda i: (i, 0)),
                   pl.BlockSpec((PT, LANES), lambda i: (i, 0)),
                   pl.BlockSpec((1, HEAD_DIM, KB), lambda i: (i, 0, 0))),
        compiler_params=_cparams(("parallel",)),
        name="prep",
    )(z, z, z, z, z, pos2, inv_a, inv_i, qg, kg)


def _f32_to_key(x):
    bits = pltpu.bitcast(x, i32)
    return bits ^ ((bits >> 31) & jnp.int32(0x7FFFFFFF))


def _key_to_f32(k):
    return pltpu.bitcast(k ^ ((k >> 31) & jnp.int32(0x7FFFFFFF)), f32)


def _attn_kernel(qT_ref, qiT_ref, wT_ref, kr_ref, kir_ref, vT_ref, o_ref, sc_ref, acc_ref, m_ref, l_ref, kn_ref):
    qb = pl.program_id(1)
    q0 = qb * QB

    @pl.when(qb == 0)
    def _():
        def kn_body(j, mx):
            kf = kr_ref[pl.ds(pl.multiple_of(j * KB, KB), KB), :].astype(f32)
            return jnp.maximum(mx, jnp.max(jnp.sum(kf * kf, axis=1, keepdims=True), axis=0, keepdims=True))

        kn = lax.fori_loop(0, kr_ref.shape[0] // KB, kn_body, jnp.zeros((1, 1), f32))
        kn_ref[...] = jnp.broadcast_to(kn, kn_ref.shape)

    nkb = (q0 + QB - 1) // KB + 1
    qidx = q0 + lax.broadcasted_iota(i32, (KB, QB), 1)
    row = lax.broadcasted_iota(i32, (KB, QB), 0)

    def score_body(j, carry):
        k0 = pl.multiple_of(j * KB, KB)
        kblk = kir_ref[pl.ds(k0, KB), 0:IDX_DIM]
        sc = jnp.zeros((KB, QB), f32)
        for hp in range(IDX_HEADS // 2):
            s = jnp.dot(kblk, qiT_ref[:, hp * 2 * QB:(hp + 1) * 2 * QB], preferred_element_type=f32)
            s = jnp.maximum(s, 0.0)
            sc = sc + s[:, 0:QB] * wT_ref[2 * hp:2 * hp + 1, :] + s[:, QB:2 * QB] * wT_ref[2 * hp + 1:2 * hp + 2, :]
        causal = k0 + row <= qidx
        sc_ref[pl.ds(k0, KB), :] = jnp.where(causal, sc, -jnp.inf)
        mn, mx, sm, sq = carry
        mn = jnp.minimum(mn, jnp.min(jnp.where(causal, sc, jnp.inf).reshape(KB // 8, 8, QB), axis=0))
        mx = jnp.maximum(mx, jnp.max(jnp.where(causal, sc, -jnp.inf).reshape(KB // 8, 8, QB), axis=0))
        s0 = jnp.where(causal, sc, 0.0)
        sm = sm + jnp.sum(s0.reshape(KB // 8, 8, QB), axis=0)
        sq = sq + jnp.sum((s0 * s0).reshape(KB // 8, 8, QB), axis=0)
        return mn, mx, sm, sq

    mn8, mx8, sm8, sq8 = lax.fori_loop(0, nkb, score_body,
                                       (jnp.full((8, QB), jnp.inf, f32), jnp.full((8, QB), -jnp.inf, f32),
                                        jnp.zeros((8, QB), f32), jnp.zeros((8, QB), f32)))

    lane1 = lax.broadcasted_iota(i32, (1, QB), 1)
    kt = jnp.minimum(TOPK_MAX, q0 + lane1 + 1)

    def count_ge(cand):
        candf = _key_to_f32(cand)

        rows = KB // COUNT_CHAINS

        def cnt_body(j, cs):
            k0 = pl.multiple_of(j * KB, KB)
            out = []
            for i, c in enumerate(cs):
                ge = (sc_ref[pl.ds(k0 + i * rows, rows), :] >= candf).astype(i32)
                out.append(c + jnp.sum(ge.reshape(rows // 8, 8, QB), axis=0))
            return tuple(out)

        cs = lax.fori_loop(0, nkb, cnt_body, tuple(jnp.zeros((8, QB), i32) for _ in range(COUNT_CHAINS)))
        c8 = (cs[0] + cs[1]) + (cs[2] + cs[3])
        return jnp.sum(c8, axis=0, keepdims=True)

    def active(lo, hi, clo):
        return (clo != kt) & (lo + 1 < hi)

    def search_cond(st):
        it, lo, hi, clo, chi, flo, fhi, side = st
        return (it < SEARCH_MAX_ITERS) & (jnp.sum(active(lo, hi, clo).astype(f32)) > 0.0)

    def search_step(st):
        it, lo, hi, clo, chi, flo, fhi, side = st
        act = active(lo, hi, clo)
        lov = _key_to_f32(lo)
        hiv = _key_to_f32(hi - 1)
        guess = _f32_to_key(lov + (hiv - lov) * (flo / (flo - fhi)))
        guess = jnp.where(it == 0, guess0, guess)
        mid = (lo >> 1) + (hi >> 1) + (lo & hi & 1)
        cand = jnp.where(it < SEARCH_INTERP_ITERS, guess, mid)
        cand = jnp.minimum(jnp.maximum(cand, lo + 1), hi - 1)
        cnt = count_ge(cand)
        up = act & (cnt >= kt)
        dn = act & (cnt < kt)
        f_new = (cnt - kt).astype(f32)
        flo2 = jnp.where(up, f_new + 0.5, jnp.where(dn & (side < 0), flo * 0.5, flo))
        fhi2 = jnp.where(dn, f_new - 0.5, jnp.where(up & (side > 0), fhi * 0.5, fhi))
        side2 = jnp.where(up, 1, jnp.where(dn, -1, side))
        return (it + 1, jnp.where(up, cand, lo), jnp.where(dn, cand, hi), jnp.where(up, cnt, clo),
                jnp.where(dn, cnt, chi), flo2, fhi2, side2)

    lo0 = _f32_to_key(jnp.min(mn8, axis=0, keepdims=True))
    hi0 = _f32_to_key(jnp.max(mx8, axis=0, keepdims=True)) + 1
    clo0 = q0 + lane1 + 1
    chi0 = jnp.zeros((1, QB), i32)
    nf = clo0.astype(f32)
    mean = jnp.sum(sm8, axis=0, keepdims=True) / nf
    std = jnp.sqrt(jnp.maximum(jnp.sum(sq8, axis=0, keepdims=True) / nf - mean * mean, 0.0))
    pt = jnp.clip(kt.astype(f32) / nf, 1e-6, 1.0 - 1e-6)
    ph = jnp.minimum(pt, 1.0 - pt)
    tq = jnp.sqrt(-2.0 * jnp.log(ph))
    zq = tq - (2.515517 + 0.802853 * tq + 0.010328 * tq * tq) / (1.0 + 1.432788 * tq + 0.189269 * tq * tq + 0.001308 * tq * tq * tq)
    guess0 = _f32_to_key(mean + jnp.where(pt <= 0.5, zq, -zq) * std)
    st = (jnp.int32(0), lo0, hi0, clo0, chi0, (clo0 - kt).astype(f32) + 0.5, (chi0 - kt).astype(f32) - 0.5,
          jnp.zeros((1, QB), i32))
    thr = lax.while_loop(search_cond, lambda s: search_step(search_step(s)), st)[1]
    thr_f = jnp.where(clo0 <= TOPK_MAX, -jnp.inf, _key_to_f32(thr))

    l_ref[...] = jnp.zeros(l_ref.shape, f32)
    acc_ref[...] = jnp.zeros(acc_ref.shape, f32)

    def block_bias(j):
        k0 = pl.multiple_of(j * KB, KB)
        sel = (sc_ref[pl.ds(k0, KB), :] >= thr_f) & (k0 + row <= qidx)
        bias1 = jnp.where(sel, 0.0, NEG).astype(f32)
        return k0, jnp.concatenate([bias1] * N_HEADS, axis=1)

    qf = qT_ref[...].astype(f32)
    qn2 = jnp.max(jnp.sum(qf * qf, axis=0, keepdims=True), axis=1, keepdims=True)
    bounded = jnp.max(qn2 * kn_ref[0:1, 0:1]) <= LOGIT_LIMIT * LOGIT_LIMIT

    def att_bounded(j, carry):
        k0, bias = block_bias(j)
        kblk = kr_ref[pl.ds(k0, KB), :]
        vt = vT_ref[j]
        p = jnp.exp2(jnp.dot(kblk, qT_ref[...], preferred_element_type=f32) + bias)
        l_ref[...] += jnp.sum(p, axis=0, keepdims=True)
        acc_ref[...] += jnp.dot(vt, p.astype(bf16), preferred_element_type=f32)
        return carry

    def att_online(j, carry):
        k0, bias = block_bias(j)
        kblk = kr_ref[pl.ds(k0, KB), :]
        vt = vT_ref[j]
        s = jnp.dot(kblk, qT_ref[...], preferred_element_type=f32) + bias
        m_old = m_ref[...]
        m_new = jnp.maximum(m_old, jnp.max(s, axis=0, keepdims=True))
        p = jnp.exp2(s - m_new)
        alpha = jnp.exp2(m_old - m_new)
        l_ref[...] = l_ref[...] * alpha + jnp.sum(p, axis=0, keepdims=True)
        m_ref[...] = m_new
        acc_ref[...] = acc_ref[...] * alpha + jnp.dot(vt, p.astype(bf16), preferred_element_type=f32)
        return carry

    @pl.when(bounded)
    def _():
        lax.fori_loop(0, nkb, att_bounded, 0)

    @pl.when(jnp.logical_not(bounded))
    def _():
        m_ref[...] = jnp.full(m_ref.shape, NEG, f32)
        lax.fori_loop(0, nkb, att_online, 0)

    out_t = acc_ref[...] / l_ref[...]
    for h in range(N_HEADS):
        o_ref[:, h * HEAD_DIM:(h + 1) * HEAD_DIM] = out_t[:, h * QB:(h + 1) * QB].T.astype(o_ref.dtype)


def _attention(qT, qiT, wT, kr, kir, vT, batch, seq):
    nq = seq // QB
    n = batch * seq
    return pl.pallas_call(
        _attn_kernel,
        out_shape=jax.ShapeDtypeStruct((n, N_HEADS * HEAD_DIM), bf16),
        grid=(batch, nq),
        in_specs=[pl.BlockSpec((None, HEAD_DIM, N_HEADS * QB), lambda b, i: (b * nq + i, 0, 0)),
                  pl.BlockSpec((None, IDX_DIM, IDX_HEADS * QB), lambda b, i: (b * nq + i, 0, 0)),
                  pl.BlockSpec((None, IDX_HEADS, QB), lambda b, i: (b * nq + i, 0, 0)),
                  pl.BlockSpec((seq, HEAD_DIM), lambda b, i: (b, 0)),
                  pl.BlockSpec((seq, LANES), lambda b, i: (b, 0)),
                  pl.BlockSpec((seq // KB, HEAD_DIM, KB), lambda b, i: (b, 0, 0))],
        out_specs=pl.BlockSpec((QB, N_HEADS * HEAD_DIM), lambda b, i: (b * nq + i, 0)),
        scratch_shapes=[pltpu.VMEM((seq, QB), f32),
                        pltpu.VMEM((HEAD_DIM, N_HEADS * QB), f32),
                        pltpu.VMEM((1, N_HEADS * QB), f32),
                        pltpu.VMEM((1, N_HEADS * QB), f32),
                        pltpu.VMEM((8, LANES), f32)],
        compiler_params=_cparams(("parallel", "arbitrary")),
        name="attn",
    )(qT, qiT, wT, kr, kir, vT)


CT = 256
HALO = 32


def _conv_kernel(u_ref, w_ref, b_ref, g_ref, beta_ref, c_ref, ybuf_ref, cbuf_ref):
    i = pl.program_id(1)

    @pl.when(i == 0)
    def _():
        ybuf_ref[0:HALO, :] = jnp.zeros((HALO, CONV_WIDTH), f32)

    @pl.when(i > 0)
    def _():
        ybuf_ref[0:HALO, :] = ybuf_ref[CT:CT + HALO, :]

    a = u_ref[:, 0:CONV_WIDTH].astype(f32)
    gate = u_ref[:, CONV_WIDTH:2 * CONV_WIDTH].astype(f32)
    ybuf_ref[HALO:HALO + CT, :] = a * jax.nn.sigmoid(gate)

    off = HALO - (CONV_KERNEL - 1)
    for c in range(CONV_WIDTH // LANES):
        cs = slice(c * LANES, (c + 1) * LANES)
        acc = jnp.broadcast_to(b_ref[:, cs], (CT, LANES))
        for ph in range(SUBLANES):
            rows = CT if ph == 0 else CT + SUBLANES
            part = None
            for o in range(ph, off + CONV_KERNEL, SUBLANES):
                if o < off:
                    continue
                term = w_ref[o - off:o - off + 1, cs] * ybuf_ref[o - ph:o - ph + rows, cs]
                part = term if part is None else part + term
            acc = acc + part[ph:ph + CT]
        cbuf_ref[:, cs] = acc

    y = cbuf_ref[...]
    mu = jnp.mean(y, axis=-1, keepdims=True)
    yc = y - mu
    var = jnp.mean(yc * yc, axis=-1, keepdims=True)
    yn = yc * lax.rsqrt(var + EPS) * g_ref[...] + beta_ref[...]
    c_ref[...] = (yn * jax.nn.sigmoid(yn)).astype(c_ref.dtype)


def _conv(z, dw_w, dw_b, ln_g, ln_b, batch, seq):
    nt = seq // CT
    n = batch * seq
    return pl.pallas_call(
        _conv_kernel,
        out_shape=jax.ShapeDtypeStruct((n, CONV_WIDTH), bf16),
        grid=(batch, nt),
        in_specs=[pl.BlockSpec((CT, 2 * CONV_WIDTH), lambda b, i: (b * nt + i, Z_U // 2048)),
                  pl.BlockSpec((32, CONV_WIDTH), lambda b, i: (0, 0)),
                  pl.BlockSpec((1, CONV_WIDTH), lambda b, i: (0, 0)),
                  pl.BlockSpec((1, CONV_WIDTH), lambda b, i: (0, 0)),
                  pl.BlockSpec((1, CONV_WIDTH), lambda b, i: (0, 0))],
        out_specs=pl.BlockSpec((CT, CONV_WIDTH), lambda b, i: (b * nt + i, 0)),
        scratch_shapes=[pltpu.VMEM((CT + HALO, CONV_WIDTH), f32), pltpu.VMEM((CT, CONV_WIDTH), f32)],
        compiler_params=_cparams(("parallel", "arbitrary")),
        name="conv",
    )(z, dw_w, dw_b, ln_g, ln_b)


MT = 256


def _mix_kernel(c_ref, a_ref, gc_ref, ga_ref, x_ref, wco_ref, wao_ref, wout_ref, g2_ref, wr_ref, br_ref,
                h1_ref, route_ref):
    yc = jnp.dot(c_ref[...], wco_ref[...], preferred_element_type=f32)
    ya = jnp.dot(a_ref[...], wao_ref[...], preferred_element_type=f32)
    mix = jax.nn.sigmoid(gc_ref[...].astype(f32)) * yc + jax.nn.sigmoid(ga_ref[...].astype(f32)) * ya
    h1 = x_ref[...] + jnp.dot(mix.astype(bf16), wout_ref[...], preferred_element_type=f32)
    h1_ref[...] = h1

    ms = jnp.mean(h1 * h1, axis=-1, keepdims=True)
    hn = h1 * lax.rsqrt(ms + EPS) * g2_ref[...]
    hn_hi = hn.astype(bf16)
    hn_lo = (hn - hn_hi.astype(f32)).astype(bf16)
    r_hi = jnp.dot(hn_hi, wr_ref[...], preferred_element_type=f32)
    r_lo = jnp.dot(hn_lo, wr_ref[:, 0:LANES], preferred_element_type=f32)
    lg = r_hi[:, 0:LANES] + (r_hi[:, LANES:2 * LANES] + r_lo) + br_ref[...]
    lane = lax.broadcasted_iota(i32, lg.shape, 1)
    big = jnp.int32(1 << 20)

    gmask = lane < N_GROUPS
    gl = jnp.where(gmask, lg, -jnp.inf)
    gmax = jnp.max(gl, axis=-1, keepdims=True)
    gidx = jnp.min(jnp.where(gl == gmax, lane, big), axis=-1, keepdims=True)
    pg = 1.0 / jnp.sum(jnp.where(gmask, jnp.exp(gl - gmax), 0.0), axis=-1, keepdims=True)

    lo = N_GROUPS + EXPERTS_PER_GROUP * gidx
    emask = (lane >= lo) & (lane < lo + EXPERTS_PER_GROUP)
    el = jnp.where(emask, lg, -jnp.inf)
    m0 = jnp.max(el, axis=-1, keepdims=True)
    i0 = jnp.min(jnp.where(el == m0, lane, big), axis=-1, keepdims=True)
    el1 = jnp.where(lane == i0, -jnp.inf, el)
    m1 = jnp.max(el1, axis=-1, keepdims=True)
    i1 = jnp.min(jnp.where(el1 == m1, lane, big), axis=-1, keepdims=True)
    e1x = jnp.exp(m1 - m0)
    w0 = 1.0 / (1.0 + e1x)
    w1 = e1x * w0

    route = jnp.where(lane == 0, (i0 - N_GROUPS).astype(f32), 0.0)
    route = jnp.where(lane == 1, (i1 - N_GROUPS).astype(f32), route)
    route = jnp.where(lane == 2, pg * w0, route)
    route = jnp.where(lane == 3, pg * w1, route)
    route_ref[...] = route


def _mix(c, a, z, x2, wco, wao, wout, g2, wr, br):
    n = x2.shape[0]
    once = functools.partial(pl.BlockSpec, pipeline_mode=pl.Buffered(1))
    return pl.pallas_call(
        _mix_kernel,
        out_shape=(jax.ShapeDtypeStruct((n, D_MODEL), f32), jax.ShapeDtypeStruct((n, LANES), f32)),
        grid=(n // MT,),
        in_specs=[pl.BlockSpec((MT, CONV_WIDTH), lambda i: (i, 0)),
                  pl.BlockSpec((MT, D_MODEL), lambda i: (i, 0)),
                  pl.BlockSpec((MT, D_MODEL), lambda i: (i, Z_GC // 2048)),
                  pl.BlockSpec((MT, D_MODEL), lambda i: (i, Z_GA // 2048)),
                  pl.BlockSpec((MT, D_MODEL), lambda i: (i, 0)),
                  once((CONV_WIDTH, D_MODEL), lambda i: (0, 0)),
                  once((D_MODEL, D_MODEL), lambda i: (0, 0)),
                  once((D_MODEL, D_MODEL), lambda i: (0, 0)),
                  pl.BlockSpec((1, D_MODEL), lambda i: (0, 0)),
                  once((D_MODEL, 2 * LANES), lambda i: (0, 0)),
                  pl.BlockSpec((1, LANES), lambda i: (0, 0))],
        out_specs=(pl.BlockSpec((MT, D_MODEL), lambda i: (i, 0)), pl.BlockSpec((MT, LANES), lambda i: (i, 0))),
        compiler_params=_cparams(("parallel",)),
        name="mix",
    )(c, a, z, z, x2, wco, wao, wout, g2, wr, br)


TR = 256
CM = 256


def _expert_kernel(te_ref, nu_ref, first_ref, wsl_ref, nxe_ref, cur_ref, nxt_ref, h_ref, g2_ref,
                   wg_hbm, wu_hbm, wd_hbm, o_ref,
                   xbuf_ref, wgf_ref, wuf_ref, wdf_ref, wgb_ref, wub_ref, wdb_ref, sem, wsem):
    t = pl.program_id(0)
    nu = nu_ref[0]
    slot = lax.rem(t, 2)

    def issue(idx_ref, dst_slot):
        def body(r, carry):
            pltpu.make_async_copy(h_ref.at[pl.ds(idx_ref[0, r], 1)], xbuf_ref.at[dst_slot, pl.ds(r, 1)],
                                  sem.at[dst_slot]).start()
            return carry

        lax.fori_loop(0, TR, body, 0, unroll=8)

    def weight_copies(e, ws):
        return (pltpu.make_async_copy(wg_hbm.at[e], wgf_ref.at[ws], wsem.at[ws]),
                pltpu.make_async_copy(wu_hbm.at[e], wuf_ref.at[ws], wsem.at[ws]),
                pltpu.make_async_copy(wd_hbm.at[e], wdf_ref.at[ws], wsem.at[ws]))

    @pl.when(t == 0)
    def _():
        for cp in weight_copies(te_ref[0], 0):
            cp.start(priority=1)
        issue(cur_ref, 0)

    @pl.when(t + 1 < nu)
    def _():
        issue(nxt_ref, 1 - slot)

    @pl.when(t < nu)
    def _():
        @pl.when(first_ref[t] == 1)
        def _():
            ws = wsl_ref[t]
            for cp in weight_copies(te_ref[t], ws):
                cp.wait()

            @pl.when(nxe_ref[t] >= 0)
            def _():
                for cp in weight_copies(nxe_ref[t], 1 - ws):
                    cp.start(priority=1)

            wgb_ref[...] = wgf_ref[ws].astype(bf16)
            wub_ref[...] = wuf_ref[ws].astype(bf16)
            wdb_ref[...] = wdf_ref[ws].astype(bf16)

        pltpu.make_async_copy(h_ref.at[pl.ds(0, TR)], xbuf_ref.at[slot], sem.at[slot]).wait()
        x = xbuf_ref[slot]
        ms = jnp.mean(x * x, axis=-1, keepdims=True)
        hn = (x * lax.rsqrt(ms + EPS) * g2_ref[...]).astype(bf16)
        gate = jnp.dot(hn, wgb_ref[...], preferred_element_type=f32)
        up = jnp.dot(hn, wub_ref[...], preferred_element_type=f32)
        act = (gate * jax.nn.sigmoid(gate) * up).astype(bf16)
        o_ref[...] = jnp.dot(act, wdb_ref[...], preferred_element_type=f32)

    @pl.when(t >= nu)
    def _():
        o_ref[...] = jnp.zeros(o_ref.shape, o_ref.dtype)


def _experts(tile_expert, n_used, first, wslot, next_expert, slot_token, h1, g2, wg, wu, wd):
    s = slot_token.shape[0]
    ntiles = s // TR
    idx3 = slot_token.reshape(ntiles, 1, TR)
    return pl.pallas_call(
        _expert_kernel,
        out_shape=jax.ShapeDtypeStruct((s, D_MODEL), f32),
        grid_spec=pltpu.PrefetchScalarGridSpec(
            num_scalar_prefetch=5,
            grid=(ntiles,),
            in_specs=[pl.BlockSpec((None, 1, TR), lambda t, *_: (t, 0, 0), memory_space=pltpu.SMEM),
                      pl.BlockSpec((None, 1, TR), lambda t, *_: (jnp.minimum(t + 1, ntiles - 1), 0, 0),
                                   memory_space=pltpu.SMEM),
                      pl.BlockSpec(memory_space=pl.ANY),
                      pl.BlockSpec((1, D_MODEL), lambda t, *_: (0, 0)),
                      pl.BlockSpec(memory_space=pl.ANY),
                      pl.BlockSpec(memory_space=pl.ANY),
                      pl.BlockSpec(memory_space=pl.ANY)],
            out_specs=pl.BlockSpec((TR, D_MODEL), lambda t, *_: (t, 0)),
            scratch_shapes=[pltpu.VMEM((2, TR, D_MODEL), f32),
                            pltpu.VMEM((2, D_MODEL, EXPERT_FF), f32),
                            pltpu.VMEM((2, D_MODEL, EXPERT_FF), f32),
                            pltpu.VMEM((2, EXPERT_FF, D_MODEL), f32),
                            pltpu.VMEM((D_MODEL, EXPERT_FF), bf16),
                            pltpu.VMEM((D_MODEL, EXPERT_FF), bf16),
                            pltpu.VMEM((EXPERT_FF, D_MODEL), bf16),
                            pltpu.SemaphoreType.DMA((2,)),
                            pltpu.SemaphoreType.DMA((2,))]),
        compiler_params=_cparams(("arbitrary",)),
        name="moe_experts",
    )(tile_expert, n_used, first, wslot, next_expert, idx3, idx3, h1, g2, wg, wu, wd)


def _combine_kernel(cur_ref, nxt_ref, h1_ref, route_ref, eo_ref, o_ref, buf_ref, sem):
    i = pl.program_id(0)
    slot = lax.rem(i, 2)

    def issue(idx_ref, dst_slot):
        def body(r, carry):
            pltpu.make_async_copy(eo_ref.at[pl.ds(idx_ref[0, r], 1)], buf_ref.at[dst_slot, 0, pl.ds(r, 1)],
                                  sem.at[dst_slot]).start()
            pltpu.make_async_copy(eo_ref.at[pl.ds(idx_ref[0, CM + r], 1)], buf_ref.at[dst_slot, 1, pl.ds(r, 1)],
                                  sem.at[dst_slot]).start()
            return carry

        lax.fori_loop(0, CM, body, 0, unroll=8)

    @pl.when(i == 0)
    def _():
        issue(cur_ref, 0)

    @pl.when(i + 1 < pl.num_programs(0))
    def _():
        issue(nxt_ref, 1 - slot)

    pltpu.make_async_copy(eo_ref.at[pl.ds(0, CM)], buf_ref.at[slot, 0], sem.at[slot]).wait()
    pltpu.make_async_copy(eo_ref.at[pl.ds(0, CM)], buf_ref.at[slot, 1], sem.at[slot]).wait()
    c0 = route_ref[:, 2:3]
    c1 = route_ref[:, 3:4]
    o_ref[...] = h1_ref[...] + c0 * buf_ref[slot, 0] + c1 * buf_ref[slot, 1]


def _combine(pos01, h1, route, eo):
    n = h1.shape[0]
    nt = n // CM
    return pl.pallas_call(
        _combine_kernel,
        out_shape=jax.ShapeDtypeStruct((n, D_MODEL), f32),
        grid=(nt,),
        in_specs=[pl.BlockSpec((None, 1, 2 * CM), lambda i: (i, 0, 0), memory_space=pltpu.SMEM),
                  pl.BlockSpec((None, 1, 2 * CM), lambda i: (jnp.minimum(i + 1, nt - 1), 0, 0),
                               memory_space=pltpu.SMEM),
                  pl.BlockSpec((CM, D_MODEL), lambda i: (i, 0)),
                  pl.BlockSpec((CM, LANES), lambda i: (i, 0)),
                  pl.BlockSpec(memory_space=pl.ANY)],
        out_specs=pl.BlockSpec((CM, D_MODEL), lambda i: (i, 0)),
        scratch_shapes=[pltpu.VMEM((2, 2, CM, D_MODEL), f32), pltpu.SemaphoreType.DMA((2,))],
        compiler_params=_cparams(("arbitrary",)),
        name="moe_combine",
    )(pos01, pos01, h1, route, eo)


def _route_tables(route, n):
    e = route[:, 0:2].astype(i32).reshape(-1)
    onehot = (e[:, None] == jnp.arange(N_EXPERTS, dtype=i32)[None, :]).astype(i32)
    csum = jnp.cumsum(onehot, axis=0)
    rank = jnp.take_along_axis(csum, e[:, None], axis=1)[:, 0] - 1
    counts = csum[-1]
    ntile = (counts + TR - 1) // TR
    tile_end = jnp.cumsum(ntile)
    tile_start = tile_end - ntile
    pos = tile_start[e] * TR + rank
    s = 2 * n + N_EXPERTS * TR
    slot_token = jnp.zeros((s,), i32).at[pos].set(jnp.arange(2 * n, dtype=i32) // 2)
    n_used = tile_end[-1:]
    t = jnp.minimum(jnp.arange(s // TR, dtype=i32), n_used[0] - 1)
    tile_expert = jnp.sum((tile_end[None, :] <= t[:, None]).astype(i32), axis=1)
    used = jnp.arange(s // TR, dtype=i32) < n_used[0]
    prev = jnp.concatenate([jnp.full((1,), -1, i32), tile_expert[:-1]])
    first = ((tile_expert != prev) & used).astype(i32)
    wslot = (jnp.cumsum(first) - 1) % 2
    nxt_tile = tile_end[tile_expert]
    next_expert = jnp.where((first == 1) & (nxt_tile < n_used[0]),
                            tile_expert[jnp.minimum(nxt_tile, s // TR - 1)], -1)
    pos01 = jnp.concatenate([pos[0::2].reshape(n // CM, 1, CM), pos[1::2].reshape(n // CM, 1, CM)], axis=2)
    return slot_token, tile_expert, n_used.astype(i32), first, wslot.astype(i32), next_expert.astype(i32), pos01


def _pack_w_in(w):
    u, q, k, v, qi, ki, wi, gc, ga = jnp.split(w, [2048, 4096, 4224, 4352, 5376, 5440, 5456, 7504], axis=1)
    pad = jnp.zeros((w.shape[0], LANES - IDX_DIM - IDX_HEADS), w.dtype)
    return jnp.concatenate([u, q, gc, ga, qi, k, v, ki, wi, pad], axis=1).astype(bf16)


def _lane_inv(rot_dim, period):
    inv = ROPE_THETA ** (-jnp.arange(0, rot_dim, 2, dtype=f32) / rot_dim)
    lane = jnp.arange(LANES)
    lp = lane % period
    return jnp.where(lp < rot_dim, inv[lp % (rot_dim // 2)], 0.0).astype(f32)[None, :]


def kernel(x, positions, attn_norm_g, w_in, conv_dw_w, conv_dw_b, conv_ln_g, conv_ln_b, w_conv_out, q_norm_g, k_norm_g, w_attn_o, w_out, ffn_norm_g, w_router_group, b_router_group, w_router_expert, b_router_expert, w_exp_gate, w_exp_up, w_exp_down):
    batch, seq, d = x.shape
    n = batch * seq
    h = x.reshape(n, d)
    pos2 = positions.reshape(n, 1)
    inv_a = _lane_inv(HEAD_DIM // ROT_FRACTION, LANES)
    inv_i = _lane_inv(IDX_DIM // ROT_FRACTION, IDX_DIM)
    for l in range(attn_norm_g.shape[0]):
        z = _inproj(h, attn_norm_g[l][None, :], _pack_w_in(w_in[l]))
        qT, qiT, wT, kr, kir, vT = _prep(z, pos2, inv_a, inv_i, q_norm_g[l][None, :], k_norm_g[l][None, :])
        a = _attention(qT, qiT, wT, kr, kir, vT, batch, seq)
        dw = jnp.concatenate([conv_dw_w[l], jnp.zeros((1, CONV_WIDTH), f32)], axis=0)
        c = _conv(z, dw, conv_dw_b[l][None, :], conv_ln_g[l][None, :], conv_ln_b[l][None, :], batch, seq)
        wr = jnp.concatenate([w_router_group[l], w_router_expert[l],
                              jnp.zeros((d, LANES - N_GROUPS - N_EXPERTS), f32)], axis=1)
        br = jnp.concatenate([b_router_group[l], b_router_expert[l],
                              jnp.zeros((LANES - N_GROUPS - N_EXPERTS,), f32)])[None, :]
        wr_hi = wr.astype(bf16)
        wr2 = jnp.concatenate([wr_hi, (wr - wr_hi.astype(f32)).astype(bf16)], axis=1)
        g2 = ffn_norm_g[l][None, :]
        h1, route = _mix(c, a, z, h, w_conv_out[l].astype(bf16), w_attn_o[l].astype(bf16), w_out[l].astype(bf16),
                         g2, wr2, br)
        slot_token, tile_expert, n_used, first, wslot, next_expert, pos01 = _route_tables(route, n)
        eo = _experts(tile_expert, n_used, first, wslot, next_expert, slot_token, h1, g2,
                      w_exp_gate[l], w_exp_up[l], w_exp_down[l])
        h = _combine(pos01, h1, route, eo)
    return h.reshape(batch, seq, d)
```
